```python
import jax, jax.numpy as jnp
from jax import lax
import numpy as np

D_MODEL = 1024
BATCH = 8
SEQ = 8192
DEPTH = 2
DEC_BATCH = 16
DEC_SEQ = 4096
PAST_LEN = 128

GRID_W = 64
NA_HEADS = 8
NA_HEAD_DIM = 64
NA_WIN_ROWS = 8
NA_WIN_COLS = 16
NA_W = NA_HEADS * NA_HEAD_DIM
CONV_CH = 512
CONV_WIDTH = 31
MLA_HEADS = 8
MLA_NOPE = 64
MLA_ROPE = 32
MLA_V = 64
Q_LORA = 256
KV_LORA = 128
ROPE_THETA = 10000.0
ATTN_BLOCK = 128
MLA_W = MLA_HEADS * MLA_V
FNET_GROUPS = 4
FNET_GROUP_DIM = 128
FNET_W = FNET_GROUPS * FNET_GROUP_DIM
PLE_DIM = 256
EPS = 1e-6

N_EVEN = (DEPTH + 1) // 2
N_ODD = DEPTH // 2
EVEN_SPLITS = (NA_W, NA_W, NA_W, NA_W, CONV_CH, CONV_CH, CONV_CH)
ODD_SPLITS = (Q_LORA, KV_LORA, MLA_ROPE, MLA_W, FNET_W, FNET_W)
EVEN_IN = sum(EVEN_SPLITS)
ODD_IN = sum(ODD_SPLITS)

kernel_name = "hybrid_natten_conformer_mla_fnet_encoder"


def _split(z, sizes):
    idx = [int(c) for c in np.cumsum(sizes)[:-1]]
    return jnp.split(z, idx, axis=-1)


def _rmsnorm(x, g):
    xf = x.astype(jnp.float32)
    y = xf * lax.rsqrt(jnp.mean(xf * xf, axis=-1, keepdims=True) + EPS)
    return (y * g.astype(jnp.float32)).astype(x.dtype)


def _layernorm(x, g, b):
    xf = x.astype(jnp.float32)
    mu = jnp.mean(xf, axis=-1, keepdims=True)
    var = jnp.mean(jnp.square(xf - mu), axis=-1, keepdims=True)
    y = (xf - mu) * lax.rsqrt(var + EPS)
    return (y * g.astype(jnp.float32) + b.astype(jnp.float32)).astype(x.dtype)


def _rope(x, cos, sin):
    xf = x.astype(jnp.float32)
    x1, x2 = jnp.split(xf, 2, axis=-1)
    return jnp.concatenate([x1 * cos - x2 * sin, x2 * cos + x1 * sin], axis=-1).astype(x.dtype)


def _na_indices(rows):
    kh = min(NA_WIN_ROWS, rows)
    r = np.arange(rows)
    rs = np.clip(r - kh // 2, 0, rows - kh)
    key_r = rs[:, None] + np.arange(kh)[None, :]
    c = np.arange(GRID_W)
    cs = np.clip(c - NA_WIN_COLS // 2, 0, GRID_W - NA_WIN_COLS)
    key_c = cs[:, None] + np.arange(NA_WIN_COLS)[None, :]
    idx = (key_r[:, None, :, None] * GRID_W + key_c[None, :, None, :]).reshape(rows, GRID_W, kh * NA_WIN_COLS)
    dr = key_r - r[:, None] + NA_WIN_ROWS - 1
    dc = key_c - c[:, None] + NA_WIN_COLS - 1
    return (jnp.asarray(idx, jnp.int32), jnp.asarray(dr, jnp.int32), jnp.asarray(dc, jnp.int32))


def _neighbourhood_attention(q, k, v, rpb):
    b, s, h, dh = q.shape
    rows = s // GRID_W
    idx, dr, dc = _na_indices(rows)
    n_keys = idx.shape[-1]
    scale = NA_HEAD_DIM ** -0.5
    q_rows = jnp.moveaxis(q.reshape(b, rows, GRID_W, h, dh), 1, 0)

    def row(args):
        q_r, idx_r, dr_r = args
        kg = jnp.take(k, idx_r, axis=1)
        vg = jnp.take(v, idx_r, axis=1)
        bias = rpb[:, dr_r[None, :, None], dc[:, None, :]].reshape(h, GRID_W, n_keys)
        sc = jnp.einsum('bwhd,bwlhd->bhwl', q_r, kg).astype(jnp.float32) * scale + bias.astype(jnp.float32)[None]
        a = jax.nn.softmax(sc, axis=-1).astype(v.dtype)
        return jnp.einsum('bhwl,bwlhd->bwhd', a, vg)

    out = lax.map(row, (q_rows, idx, dr))
    return jnp.moveaxis(out, 0, 1).reshape(b, s, h * dh)


def _mla_attention(qn, qr, kn, kr, v):
    b, s, h, _ = qn.shape
    nb = s // ATTN_BLOCK
    scale = (MLA_NOPE + MLA_ROPE) ** -0.5
    qn_b = jnp.moveaxis(qn.reshape(b, nb, ATTN_BLOCK, h, MLA_NOPE), 1, 0)
    qr_b = jnp.moveaxis(qr.reshape(b, nb, ATTN_BLOCK, h, MLA_ROPE), 1, 0)

    def blk(args):
        qn_i, qr_i = args
        sc = (jnp.einsum('bqhd,bkhd->bhqk', qn_i, kn)
              + jnp.einsum('bqhr,bkr->bhqk', qr_i, kr)).astype(jnp.float32) * scale
        a = jax.nn.softmax(sc, axis=-1).astype(v.dtype)
        return jnp.einsum('bhqk,bkhd->bqhd', a, v)

    out = lax.map(blk, (qn_b, qr_b))
    return jnp.moveaxis(out, 0, 1).reshape(b, s, h * MLA_V)


def _even_mixer(h, w_in, rpb, dw_w, dw_b, cln_g, cln_b, w_out):
    b, s, _ = h.shape
    z = h @ w_in
    q, k, v, g_a, u_a, u_b, g_b = _split(z, EVEN_SPLITS)
    shp = (b, s, NA_HEADS, NA_HEAD_DIM)
    a_out = _neighbourhood_attention(q.reshape(shp), k.reshape(shp), v.reshape(shp), rpb)
    a_out = a_out * jax.nn.silu(g_a)
    u = u_a * jax.nn.sigmoid(u_b)
    u = lax.conv_general_dilated(u, dw_w[:, None, :], window_strides=(1,),
                                 padding=[(CONV_WIDTH // 2, CONV_WIDTH // 2)],
                                 dimension_numbers=('NWC', 'WIO', 'NWC'),
                                 feature_group_count=CONV_CH) + dw_b
    u = jax.nn.silu(_layernorm(u, cln_g, cln_b)) * jax.nn.silu(g_b)
    return jnp.concatenate([a_out, u], axis=-1) @ w_out


def _odd_mixer(h, w_in, q_norm_g, kv_norm_g, w_uq, w_ukv, w_out):
    b, s, _ = h.shape
    z = h @ w_in
    c_q, c_kv, k_rope, g_c, f_in, g_d = _split(z, ODD_SPLITS)
    pos = jnp.arange(s, dtype=jnp.float32)
    inv_freq = ROPE_THETA ** (-jnp.arange(0, MLA_ROPE, 2, dtype=jnp.float32) / MLA_ROPE)
    ang = pos[:, None] * inv_freq[None, :]
    cos, sin = jnp.cos(ang), jnp.sin(ang)
    q = (_rmsnorm(c_q, q_norm_g) @ w_uq).reshape(b, s, MLA_HEADS, MLA_NOPE + MLA_ROPE)
    qn, qr = q[..., :MLA_NOPE], q[..., MLA_NOPE:]
    qr = _rope(qr, cos[:, None, :], sin[:, None, :])
    kv = (_rmsnorm(c_kv, kv_norm_g) @ w_ukv).reshape(b, s, MLA_HEADS, MLA_NOPE + MLA_V)
    kn, vv = kv[..., :MLA_NOPE], kv[..., MLA_NOPE:]
    kr = _rope(k_rope, cos, sin)
    c_out = _mla_attention(qn, qr, kn, kr, vv) * jax.nn.silu(g_c)
    f = f_in.reshape(b, s, FNET_GROUPS, FNET_GROUP_DIM).astype(jnp.float32)
    f = jnp.fft.fft2(f, axes=(1, 3), norm='ortho').real.reshape(b, s, FNET_W).astype(h.dtype)
    d_out = f * jax.nn.silu(g_d)
    return jnp.concatenate([c_out, d_out], axis=-1) @ w_out


def _trunk(x, p, g_pre, g_post, w_ple, w_ple_gate,
           w_in_e, rpb, dw_w, dw_b, cln_g, cln_b, w_out_e,
           w_in_o, q_norm_g, kv_norm_g, w_uq, w_ukv, w_out_o):
    for i in range(DEPTH):
        h = _rmsnorm(x, g_pre[i])
        j = i // 2
        if i % 2 == 0:
            o = _even_mixer(h, w_in_e[j], rpb[j], dw_w[j], dw_b[j], cln_g[j], cln_b[j], w_out_e[j])
        else:
            o = _odd_mixer(h, w_in_o[j], q_norm_g[j], kv_norm_g[j], w_uq[j], w_ukv[j], w_out_o[j])
        x = x + _rmsnorm(o, g_post[i])
        x = x + jax.nn.sigmoid(x @ w_ple_gate[i]) * (p[i] @ w_ple[i])
    return x


def setup_inputs(seed: int = 0) -> dict:
    key = jax.random.key(seed)
    ks = jax.random.split(key, 24)
    f32 = jnp.float32

    def nrm(k, shape, fan_in):
        return jax.random.normal(k, shape, f32) * (fan_in ** -0.5)

    def gain(k, shape):
        return 1.0 + 0.02 * jax.random.normal(k, shape, f32)

    return {
        "x_prompt": jax.random.normal(ks[0], (BATCH, SEQ, D_MODEL), f32),
        "x_sample": jax.random.normal(ks[1], (DEC_BATCH, DEC_SEQ, D_MODEL), f32),
        "p_prompt": jax.random.normal(ks[2], (DEPTH, BATCH, SEQ, PLE_DIM), f32),
        "p_sample": jax.random.normal(ks[3], (DEPTH, DEC_BATCH, DEC_SEQ, PLE_DIM), f32),
        "g_pre": gain(ks[4], (DEPTH, D_MODEL)),
        "g_post": gain(ks[5], (DEPTH, D_MODEL)),
        "w_ple": nrm(ks[6], (DEPTH, PLE_DIM, D_MODEL), PLE_DIM),
        "w_ple_gate": nrm(ks[7], (DEPTH, D_MODEL, D_MODEL), D_MODEL),
        "w_in_e": nrm(ks[8], (N_EVEN, D_MODEL, EVEN_IN), D_MODEL),
        "rpb": 0.1 * jax.random.normal(ks[9], (N_EVEN, NA_HEADS, 2 * NA_WIN_ROWS - 1, 2 * NA_WIN_COLS - 1), f32),
        "dw_w": nrm(ks[10], (N_EVEN, CONV_WIDTH, CONV_CH), CONV_WIDTH),
        "dw_b": 0.02 * jax.random.normal(ks[11], (N_EVEN, CONV_CH), f32),
        "cln_g": gain(ks[12], (N_EVEN, CONV_CH)),
        "cln_b": 0.02 * jax.random.normal(ks[13], (N_EVEN, CONV_CH), f32),
        "w_out_e": nrm(ks[14], (N_EVEN, NA_W + CONV_CH, D_MODEL), NA_W + CONV_CH),
        "w_in_o": nrm(ks[15], (N_ODD, D_MODEL, ODD_IN), D_MODEL),
        "q_norm_g": gain(ks[16], (N_ODD, Q_LORA)),
        "kv_norm_g": gain(ks[17], (N_ODD, KV_LORA)),
        "w_uq": nrm(ks[18], (N_ODD, Q_LORA, MLA_HEADS * (MLA_NOPE + MLA_ROPE)), Q_LORA),
        "w_ukv": nrm(ks[19], (N_ODD, KV_LORA, MLA_HEADS * (MLA_NOPE + MLA_V)), KV_LORA),
        "w_out_o": nrm(ks[20], (N_ODD, MLA_W + FNET_W, D_MODEL), MLA_W + FNET_W),
    }


def reference(x_prompt, x_sample, p_prompt, p_sample, g_pre, g_post, w_ple, w_ple_gate,
              w_in_e, rpb, dw_w, dw_b, cln_g, cln_b, w_out_e,
              w_in_o, q_norm_g, kv_norm_g, w_uq, w_ukv, w_out_o):
    y_prompt = _trunk(x_prompt, p_prompt, g_pre, g_post, w_ple, w_ple_gate,
                      w_in_e, rpb, dw_w, dw_b, cln_g, cln_b, w_out_e,
                      w_in_o, q_norm_g, kv_norm_g, w_uq, w_ukv, w_out_o)
    y_sample = _trunk(x_sample, p_sample, g_pre, g_post, w_ple, w_ple_gate,
                      w_in_e, rpb, dw_w, dw_b, cln_g, cln_b, w_out_e,
                      w_in_o, q_norm_g, kv_norm_g, w_uq, w_ukv, w_out_o)
    return (y_prompt, y_sample)
```

```python
import functools

import numpy as np
import jax
import jax.numpy as jnp
from jax import lax
from jax.experimental import pallas as pl
from jax.experimental.pallas import tpu as pltpu

F32 = jnp.float32
BF16 = jnp.bfloat16

D_MODEL = 1024
GRID_W = 64
NA_HEADS = 8
NA_HEAD_DIM = 64
NA_WIN_ROWS = 8
NA_WIN_COLS = 16
NA_W = NA_HEADS * NA_HEAD_DIM
CONV_CH = 512
CONV_WIDTH = 31
MLA_HEADS = 8
MLA_NOPE = 64
MLA_ROPE = 32
MLA_V = 64
Q_LORA = 256
KV_LORA = 128
ROPE_THETA = 10000.0
FNET_GROUPS = 4
FNET_GROUP_DIM = 128
PLE_DIM = 256
EPS = 1e-6

LANES = 128
HALO = 16
NEG = -1e30
LOG2E = 1.4426950408889634
VMEM_LIMIT = 56 * 1024 * 1024

TM = 512
NA_ROWS_PER_STEP = 8
CONV_TS = 256
CONV_RC = 32
MLA_TQ = 512
MLA_TK = 512
FNET_N2 = 128


def _cparams(sem):
    return pltpu.CompilerParams(dimension_semantics=sem, vmem_limit_bytes=VMEM_LIMIT)


def _dot(a, b):
    return jnp.dot(a, b, preferred_element_type=F32)


def _dot_nt(a, b):
    return lax.dot_general(a, b, (((1,), (1,)), ((), ())), preferred_element_type=F32)


def _silu(x):
    return x * jax.nn.sigmoid(x)


def _rms(x, g):
    return x * lax.rsqrt(jnp.mean(x * x, axis=-1, keepdims=True) + EPS) * g


def _in_even_kernel(x_ref, g_ref, w_ref, q_ref, k_ref, v_ref, ga_ref, u_ref, gb_ref):
    h = _rms(x_ref[...], g_ref[...]).astype(BF16)

    def proj(c):
        return _dot(h, w_ref[:, c * NA_W:(c + 1) * NA_W])

    q_ref[...] = (proj(0) * (NA_HEAD_DIM ** -0.5)).astype(BF16)
    k_ref[...] = proj(1).astype(BF16)
    v_ref[...] = proj(2).astype(BF16)
    ga_ref[...] = _silu(proj(3)).astype(BF16)
    u_ref[...] = (proj(4) * jax.nn.sigmoid(proj(5))).astype(BF16)
    gb_ref[...] = _silu(proj(6)).astype(BF16)


def _in_even(x2, g, w):
    t = x2.shape[0]
    n_in = w.shape[1]
    tok = lambda i: (i, 0)
    fixed = lambda i: (0, 0)
    out = jax.ShapeDtypeStruct((t, NA_W), BF16)
    return pl.pallas_call(
        _in_even_kernel,
        grid=(t // TM,),
        in_specs=[pl.BlockSpec((TM, D_MODEL), tok), pl.BlockSpec((1, D_MODEL), fixed),
                  pl.BlockSpec((D_MODEL, n_in), fixed)],
        out_specs=[pl.BlockSpec((TM, NA_W), tok)] * 6,
        out_shape=[out] * 6,
        compiler_params=_cparams(("parallel",)),
        name="in_even",
    )(x2, g, w)


def _na_bias_table(rpb):
    w = np.arange(GRID_W)
    cs = np.clip(w - NA_WIN_COLS // 2, 0, GRID_W - NA_WIN_COLS)
    kc = np.arange(GRID_W)
    inside = (kc[None, :] >= cs[:, None]) & (kc[None, :] < cs[:, None] + NA_WIN_COLS)
    dc = np.clip(kc[None, :] - w[:, None] + NA_WIN_COLS - 1, 0, 2 * NA_WIN_COLS - 2)
    g = rpb[:, :, dc]
    g = jnp.where(jnp.asarray(inside)[None, None], g, NEG)
    pair = jnp.concatenate([g[:, :-1], g[:, 1:]], axis=-1)
    return pair.reshape(NA_HEADS * (2 * NA_WIN_ROWS - 2), GRID_W, 2 * GRID_W).astype(F32)


def _na_kernel(q_ref, kp_ref, kc_ref, kn_ref, vp_ref, vc_ref, vn_ref, ga_ref, tb_ref, o_ref,
               ks_ref, vs_ref, *, rows):
    blk = NA_ROWS_PER_STEP * GRID_W
    i = pl.program_id(1)
    ks_ref[0:blk] = kp_ref[0]
    ks_ref[blk:2 * blk] = kc_ref[0]
    ks_ref[2 * blk:3 * blk] = kn_ref[0]
    vs_ref[0:blk] = vp_ref[0]
    vs_ref[blk:2 * blk] = vc_ref[0]
    vs_ref[2 * blk:3 * blk] = vn_ref[0]
    lo = lax.broadcasted_iota(jnp.int32, (GRID_W, LANES), 1) < NA_HEAD_DIM
    n_keys = NA_WIN_ROWS * GRID_W

    def row_body(jr, carry):
        r = i * NA_ROWS_PER_STEP + jr
        rs = jnp.clip(r - NA_WIN_ROWS // 2, 0, rows - NA_WIN_ROWS)
        cls = r - rs
        start = pl.multiple_of((rs - (i - 1) * NA_ROWS_PER_STEP) * GRID_W, GRID_W)
        q0 = pl.multiple_of(jr * GRID_W, GRID_W)
        outs = []
        for pr in range(NA_HEADS // 2):
            cols = slice(pr * LANES, (pr + 1) * LANES)
            qp = q_ref[0, pl.ds(q0, GRID_W), cols]
            kblk = ks_ref[pl.ds(start, n_keys), cols]
            vblk = vs_ref[pl.ds(start, n_keys), cols]
            o_h = []
            for hh in range(2):
                head = 2 * pr + hh
                qm = jnp.where(lo if hh == 0 else jnp.logical_not(lo), qp, jnp.zeros_like(qp))
                s = _dot_nt(qm, kblk)
                base = head * (2 * NA_WIN_ROWS - 2) + (NA_WIN_ROWS - 1) - cls
                bias = jnp.concatenate([tb_ref[base + 2 * m] for m in range(NA_WIN_ROWS // 2)], axis=1)
                s = s + bias
                mx = jnp.max(s, axis=1, keepdims=True)
                p = jnp.exp(s - mx)
                den = jnp.sum(p, axis=1, keepdims=True)
                o_h.append(_dot(p.astype(BF16), vblk) / den)
            outs.append(jnp.where(lo, o_h[0], o_h[1]))
        o = jnp.concatenate(outs, axis=1)
        ga = ga_ref[0, pl.ds(q0, GRID_W), :].astype(F32)
        o_ref[0, pl.ds(q0, GRID_W), :] = (o * ga).astype(BF16)
        return carry

    lax.fori_loop(0, NA_ROWS_PER_STEP, row_body, 0)


def _natten(q, k, v, ga, table):
    b, s, _ = q.shape
    rows = s // GRID_W
    assert rows % NA_ROWS_PER_STEP == 0 and rows >= NA_WIN_ROWS
    nblk = rows // NA_ROWS_PER_STEP
    blk = NA_ROWS_PER_STEP * GRID_W
    cur = lambda bi, i: (bi, i, 0)
    prev = lambda bi, i: (bi, jnp.maximum(i - 1, 0), 0)
    nxt = lambda bi, i: (bi, jnp.minimum(i + 1, nblk - 1), 0)
    spec = lambda im: pl.BlockSpec((1, blk, NA_W), im)
    return pl.pallas_call(
        functools.partial(_na_kernel, rows=rows),
        grid=(b, nblk),
        in_specs=[spec(cur), spec(prev), spec(cur), spec(nxt), spec(prev), spec(cur), spec(nxt), spec(cur),
                  pl.BlockSpec(table.shape, lambda bi, i: (0, 0, 0))],
        out_specs=spec(cur),
        out_shape=jax.ShapeDtypeStruct((b, s, NA_W), BF16),
        scratch_shapes=[pltpu.VMEM((3 * blk, NA_W), BF16), pltpu.VMEM((3 * blk, NA_W), BF16)],
        compiler_params=_cparams(("parallel", "parallel")),
        name="natten",
    )(q, k, k, k, v, v, v, ga, table)


def _conv_kernel(up_ref, uc_ref, un_ref, gb_ref, w_ref, b_ref, lg_ref, lb_ref, o_ref, slab_ref, *, nblk):
    i = pl.program_id(1)
    ts = CONV_TS
    prev = up_ref[0].astype(F32)
    nxt = un_ref[0].astype(F32)
    slab_ref[0:HALO] = jnp.where(i > 0, prev, jnp.zeros_like(prev))
    slab_ref[HALO:HALO + ts] = uc_ref[0].astype(F32)
    slab_ref[HALO + ts:2 * HALO + ts] = jnp.where(i < nblk - 1, nxt, jnp.zeros_like(nxt))
    first = HALO - CONV_WIDTH // 2
    for rc in range(ts // CONV_RC):
        r0 = rc * CONV_RC
        acc = jnp.broadcast_to(b_ref[...], (CONV_RC, CONV_CH))
        for tap in range(CONV_WIDTH):
            lo = r0 + first + tap
            acc = acc + w_ref[tap:tap + 1, :] * slab_ref[lo:lo + CONV_RC, :]
        mu = jnp.mean(acc, axis=-1, keepdims=True)
        cen = acc - mu
        var = jnp.mean(cen * cen, axis=-1, keepdims=True)
        y = cen * lax.rsqrt(var + EPS) * lg_ref[...] + lb_ref[...]
        gb = gb_ref[0, r0:r0 + CONV_RC, :].astype(F32)
        o_ref[0, r0:r0 + CONV_RC, :] = (_silu(y) * gb).astype(BF16)


def _conv_branch(u, gb, dw_w, dw_b, cln_g, cln_b):
    b, s, _ = u.shape
    ts = CONV_TS
    nblk = s // ts
    per = ts // HALO
    cur = lambda bi, i: (bi, i, 0)
    fixed = lambda bi, i: (0, 0)
    halo_prev = lambda bi, i: (bi, jnp.maximum(i * per - 1, 0), 0)
    halo_next = lambda bi, i: (bi, jnp.minimum((i + 1) * per, s // HALO - 1), 0)
    vec = pl.BlockSpec((1, CONV_CH), fixed)
    return pl.pallas_call(
        functools.partial(_conv_kernel, nblk=nblk),
        grid=(b, nblk),
        in_specs=[pl.BlockSpec((1, HALO, CONV_CH), halo_prev), pl.BlockSpec((1, ts, CONV_CH), cur),
                  pl.BlockSpec((1, HALO, CONV_CH), halo_next), pl.BlockSpec((1, ts, CONV_CH), cur),
                  pl.BlockSpec((CONV_WIDTH, CONV_CH), fixed), vec, vec, vec],
        out_specs=pl.BlockSpec((1, ts, CONV_CH), cur),
        out_shape=jax.ShapeDtypeStruct((b, s, CONV_CH), BF16),
        scratch_shapes=[pltpu.VMEM((ts + 2 * HALO, CONV_CH), F32)],
        compiler_params=_cparams(("parallel", "parallel")),
        name="conv_branch",
    )(u, u, u, gb, dw_w, dw_b, cln_g, cln_b)


def _tail_kernel(a_ref, c_ref, x_ref, p_ref, wo_ref, g_ref, wg_ref, wp_ref, o_ref):
    half = a_ref.shape[1]
    o = _dot(a_ref[...], wo_ref[0:half, :]) + _dot(c_ref[...], wo_ref[half:2 * half, :])
    x1 = x_ref[...] + _rms(o, g_ref[...])
    gate = jax.nn.sigmoid(_dot(x1.astype(BF16), wg_ref[...]))
    pe = _dot(p_ref[...].astype(BF16), wp_ref[...])
    o_ref[...] = x1 + gate * pe


def _tail(a, c, x2, p, layer, w_out, g_post, w_gate, w_ple):
    t = x2.shape[0]
    half = a.shape[1]
    tok = lambda i: (i, 0)
    fixed = lambda i: (0, 0)
    return pl.pallas_call(
        _tail_kernel,
        grid=(t // TM,),
        in_specs=[pl.BlockSpec((TM, half), tok), pl.BlockSpec((TM, half), tok), pl.BlockSpec((TM, D_MODEL), tok),
                  pl.BlockSpec((None, TM, PLE_DIM), lambda i: (layer, i, 0)),
                  pl.BlockSpec((2 * half, D_MODEL), fixed), pl.BlockSpec((1, D_MODEL), fixed),
                  pl.BlockSpec((D_MODEL, D_MODEL), fixed), pl.BlockSpec((PLE_DIM, D_MODEL), fixed)],
        out_specs=pl.BlockSpec((TM, D_MODEL), tok),
        out_shape=jax.ShapeDtypeStruct((t, D_MODEL), F32),
        compiler_params=_cparams(("parallel",)),
        name="layer_tail",
    )(a, c, x2, p, w_out, g_post, w_gate, w_ple)


_O_CQ = 0
_O_CKV = _O_CQ + Q_LORA
_O_KR = _O_CKV + KV_LORA
_O_KRS = _O_KR + LANES
_O_GC = _O_KRS + LANES
_O_F = _O_GC + MLA_HEADS * MLA_V
_O_GD = _O_F + FNET_GROUPS * FNET_GROUP_DIM
_O_END = _O_GD + FNET_GROUPS * FNET_GROUP_DIM
HEAD_PAD = LANES


def _in_odd_kernel(x_ref, g_ref, w_ref, qg_ref, kvg_ref, wuq_ref, wuqs_ref, wuk_ref, wuv_ref, rope_ref, dft_ref,
                   q_ref, k_ref, v_ref, gc_ref, zr_ref, zi_ref, gd_ref):
    h = _rms(x_ref[...], g_ref[...]).astype(BF16)

    def proj(lo, hi):
        return _dot(h, w_ref[:, lo:hi])

    cos_q = rope_ref[:, 0:LANES]
    sin_q = rope_ref[:, LANES:2 * LANES]
    cos_k = rope_ref[:, 2 * LANES:3 * LANES]
    sin_k = rope_ref[:, 3 * LANES:4 * LANES]

    cq = _rms(proj(_O_CQ, _O_CKV), qg_ref[...]).astype(BF16)
    qm = _dot(cq, wuq_ref[...])
    qs = _dot(cq, wuqs_ref[...])
    for hd in range(MLA_HEADS):
        cols = slice(hd * HEAD_PAD, (hd + 1) * HEAD_PAD)
        q_ref[:, cols] = (qm[:, cols] * cos_q + qs[:, cols] * sin_q).astype(BF16)

    ckv = _rms(proj(_O_CKV, _O_KR), kvg_ref[...]).astype(BF16)
    krot = proj(_O_KR, _O_KRS) * cos_k + proj(_O_KRS, _O_GC) * sin_k
    kn = _dot(ckv, wuk_ref[...])
    for hd in range(MLA_HEADS):
        cols = slice(hd * HEAD_PAD, (hd + 1) * HEAD_PAD)
        k_ref[:, cols] = (kn[:, cols] + krot).astype(BF16)
    v_ref[...] = _dot(ckv, wuv_ref[...]).astype(BF16)

    gc_ref[...] = _silu(proj(_O_GC, _O_F)).astype(BF16)
    gd_ref[...] = _silu(proj(_O_GD, _O_END)).astype(BF16)
    f = proj(_O_F, _O_GD).astype(BF16)
    for grp in range(FNET_GROUPS):
        cols = slice(grp * FNET_GROUP_DIM, (grp + 1) * FNET_GROUP_DIM)
        z = _dot(f[:, cols], dft_ref[...])
        zr_ref[:, cols] = z[:, 0:FNET_GROUP_DIM].astype(BF16)
        zi_ref[:, cols] = z[:, FNET_GROUP_DIM:2 * FNET_GROUP_DIM].astype(BF16)


def _in_odd(x2, seq, g, w, qg, kvg, wuq, wuqs, wuk, wuv, rope_tab, dft_c):
    t = x2.shape[0]
    per_seq = seq // TM
    tok = lambda i: (i, 0)
    fixed = lambda i: (0, 0)
    full = lambda a: pl.BlockSpec(a.shape, fixed)
    o1024 = jax.ShapeDtypeStruct((t, MLA_HEADS * HEAD_PAD), BF16)
    o512 = jax.ShapeDtypeStruct((t, 512), BF16)
    s1024 = pl.BlockSpec((TM, MLA_HEADS * HEAD_PAD), tok)
    s512 = pl.BlockSpec((TM, 512), tok)
    return pl.pallas_call(
        _in_odd_kernel,
        grid=(t // TM,),
        in_specs=[pl.BlockSpec((TM, D_MODEL), tok), full(g), full(w), full(qg), full(kvg), full(wuq), full(wuqs),
                  full(wuk), full(wuv), pl.BlockSpec((TM, 4 * LANES), lambda i: (i % per_seq, 0)), full(dft_c)],
        out_specs=[s1024, s1024, s512, s512, s512, s512, s512],
        out_shape=[o1024, o1024, o512, o512, o512, o512, o512],
        compiler_params=_cparams(("parallel",)),
        name="in_odd",
    )(x2, g, w, qg, kvg, wuq, wuqs, wuk, wuv, rope_tab, dft_c)


def _mla_kernel(q_ref, k_ref, v_ref, gc_ref, o_ref, m_ref, l_ref, acc_ref):
    kt = pl.program_id(2)

    @pl.when(kt == 0)
    def _():
        m_ref[...] = jnp.full(m_ref.shape, NEG, F32)
        l_ref[...] = jnp.zeros(l_ref.shape, F32)
        acc_ref[...] = jnp.zeros(acc_ref.shape, F32)

    for hd in range(MLA_HEADS):
        cols = slice(hd * HEAD_PAD, (hd + 1) * HEAD_PAD)
        s = _dot_nt(q_ref[0, :, cols], k_ref[0, :, cols])
        m_prev = m_ref[hd]
        m_new = jnp.maximum(m_prev, jnp.max(s, axis=1, keepdims=True))
        alpha = jnp.exp2(m_prev - m_new)
        p = jnp.exp2(s - m_new)
        l_ref[hd] = alpha * l_ref[hd] + jnp.sum(p, axis=1, keepdims=True)
        vcols = slice((hd // 2) * LANES, (hd // 2 + 1) * LANES)
        acc_ref[hd] = alpha * acc_ref[hd] + _dot(p.astype(BF16), v_ref[0, :, vcols])
        m_ref[hd] = m_new

    @pl.when(kt == pl.num_programs(2) - 1)
    def _():
        lo = lax.broadcasted_iota(jnp.int32, (q_ref.shape[1], LANES), 1) < MLA_V
        for pr in range(MLA_HEADS // 2):
            oa = acc_ref[2 * pr] / l_ref[2 * pr]
            ob = acc_ref[2 * pr + 1] / l_ref[2 * pr + 1]
            cols = slice(pr * LANES, (pr + 1) * LANES)
            o_ref[0, :, cols] = (jnp.where(lo, oa, ob) * gc_ref[0, :, cols].astype(F32)).astype(BF16)


def _mla(q, k, v, gc):
    b, s, _ = q.shape
    tq, tk = min(MLA_TQ, s), min(MLA_TK, s)
    qmap = lambda bi, qi, ki: (bi, qi, 0)
    kmap = lambda bi, qi, ki: (bi, ki, 0)
    return pl.pallas_call(
        _mla_kernel,
        grid=(b, s // tq, s // tk),
        in_specs=[pl.BlockSpec((1, tq, MLA_HEADS * HEAD_PAD), qmap), pl.BlockSpec((1, tk, MLA_HEADS * HEAD_PAD), kmap),
                  pl.BlockSpec((1, tk, MLA_HEADS * MLA_V), kmap), pl.BlockSpec((1, tq, MLA_HEADS * MLA_V), qmap)],
        out_specs=pl.BlockSpec((1, tq, MLA_HEADS * MLA_V), qmap),
        out_shape=jax.ShapeDtypeStruct((b, s, MLA_HEADS * MLA_V), BF16),
        scratch_shapes=[pltpu.VMEM((MLA_HEADS, tq, 1), F32), pltpu.VMEM((MLA_HEADS, tq, 1), F32),
                        pltpu.VMEM((MLA_HEADS, tq, LANES), F32)],
        compiler_params=_cparams(("parallel", "parallel", "arbitrary")),
        name="mla_attention",
    )(q, k, v, gc)


FNET_G = 8


def _fnet1_kernel(zr_ref, zi_ref, m1_ref, twc_ref, tws_ref, br_ref, bi_ref):
    n1 = zr_ref.shape[1]
    width = FNET_GROUPS * FNET_GROUP_DIM
    a = _dot(m1_ref[...], jnp.concatenate([zr_ref[0], zi_ref[0]], axis=0))
    ar, ai = a[0:n1], a[n1:2 * n1]
    for j in range(FNET_G):
        tc = jnp.concatenate([twc_ref[j]] * (width // LANES), axis=1)
        ts = jnp.concatenate([tws_ref[j]] * (width // LANES), axis=1)
        cols = slice(j * width, (j + 1) * width)
        br_ref[0, :, cols] = (ar[:, cols] * tc + ai[:, cols] * ts).astype(BF16)
        bi_ref[0, :, cols] = (ai[:, cols] * tc - ar[:, cols] * ts).astype(BF16)


def _fnet2_kernel(br_ref, bi_ref, gd_ref, m2_ref, o_ref):
    width = FNET_GROUPS * FNET_GROUP_DIM
    for j in range(FNET_G):
        y = _dot(m2_ref[...], jnp.concatenate([br_ref[0, j], bi_ref[0, j]], axis=0))
        cols = slice(j * width, (j + 1) * width)
        o_ref[0, :, cols] = (y * gd_ref[0, :, cols].astype(F32)).astype(BF16)


def _fnet_tables(seq):
    n2 = FNET_N2
    n1 = seq // n2
    k1 = np.arange(n1)
    a1 = 2.0 * np.pi * np.outer(k1, k1) / n1
    c1, s1 = np.cos(a1) / np.sqrt(n1), np.sin(a1) / np.sqrt(n1)
    m1 = np.block([[c1, s1], [-s1, c1]])
    k2 = np.arange(n2)
    a2 = 2.0 * np.pi * np.outer(k2, k2) / n2
    m2 = np.concatenate([np.cos(a2), np.sin(a2)], axis=1) / np.sqrt(n2)
    at = 2.0 * np.pi * np.outer(k2, k1) / seq
    twc = np.broadcast_to(np.cos(at)[:, :, None], (n2, n1, LANES))
    tws = np.broadcast_to(np.sin(at)[:, :, None], (n2, n1, LANES))
    return (jnp.asarray(m1, BF16), jnp.asarray(m2, BF16), jnp.asarray(twc, F32), jnp.asarray(tws, F32))


def _fnet(zr, zi, gd):
    b, s, width = zr.shape
    n2 = FNET_N2
    n1 = s // n2
    assert n1 % FNET_G == 0 and n2 % FNET_G == 0
    m1, m2, twc, tws = _fnet_tables(s)
    zr2 = zr.reshape(b, n1, n2 * width)
    zi2 = zi.reshape(b, n1, n2 * width)
    tcw = FNET_G * width
    dat = pl.BlockSpec((1, n1, tcw), lambda bi, j: (bi, 0, j))
    tw = pl.BlockSpec((FNET_G, n1, LANES), lambda bi, j: (j, 0, 0))
    bshape = jax.ShapeDtypeStruct((b, n1, n2 * width), BF16)
    br, bi_ = pl.pallas_call(
        _fnet1_kernel,
        grid=(b, n2 // FNET_G),
        in_specs=[dat, dat, pl.BlockSpec(m1.shape, lambda bi, j: (0, 0)), tw, tw],
        out_specs=[dat, dat],
        out_shape=[bshape, bshape],
        compiler_params=_cparams(("parallel", "parallel")),
        name="fnet_stage1",
    )(zr2, zi2, m1, twc, tws)
    br4 = br.reshape(b, n1, n2, width)
    bi4 = bi_.reshape(b, n1, n2, width)
    gd2 = gd.reshape(b, n2, n1 * width)
    bspec = pl.BlockSpec((1, FNET_G, n2, width), lambda bi, j: (bi, j, 0, 0))
    ospec = pl.BlockSpec((1, n2, tcw), lambda bi, j: (bi, 0, j))
    out = pl.pallas_call(
        _fnet2_kernel,
        grid=(b, n1 // FNET_G),
        in_specs=[bspec, bspec, ospec, pl.BlockSpec(m2.shape, lambda bi, j: (0, 0))],
        out_specs=ospec,
        out_shape=jax.ShapeDtypeStruct((b, n2, n1 * width), BF16),
        compiler_params=_cparams(("parallel", "parallel")),
        name="fnet_stage2",
    )(br4, bi4, gd2, m2)
    return out.reshape(b, s, width)


def _odd_weights(w_in, w_uq, w_ukv):
    half = MLA_ROPE // 2
    o_kr = Q_LORA + KV_LORA
    o_gc = o_kr + MLA_ROPE
    kr = w_in[:, o_kr:o_gc]
    kr_swapped = jnp.concatenate([-kr[:, half:], kr[:, :half]], axis=1)
    zeros = lambda n: jnp.zeros((D_MODEL, n), w_in.dtype)
    pad_kr = lambda a: jnp.concatenate([zeros(MLA_NOPE), a, zeros(HEAD_PAD - MLA_NOPE - MLA_ROPE)], axis=1)
    w = jnp.concatenate([w_in[:, :o_kr], pad_kr(kr), pad_kr(kr_swapped), w_in[:, o_gc:]], axis=1)
    assert w.shape[1] == _O_END

    uq = w_uq.reshape(Q_LORA, MLA_HEADS, MLA_NOPE + MLA_ROPE)
    rope = uq[:, :, MLA_NOPE:]
    rope_swapped = jnp.concatenate([-rope[:, :, half:], rope[:, :, :half]], axis=2)
    zpad = jnp.zeros((Q_LORA, MLA_HEADS, HEAD_PAD - MLA_NOPE - MLA_ROPE), w_uq.dtype)
    znope = jnp.zeros((Q_LORA, MLA_HEADS, MLA_NOPE), w_uq.dtype)
    wuq = jnp.concatenate([uq, zpad], axis=2).reshape(Q_LORA, MLA_HEADS * HEAD_PAD)
    wuqs = jnp.concatenate([znope, rope_swapped, zpad], axis=2).reshape(Q_LORA, MLA_HEADS * HEAD_PAD)

    ukv = w_ukv.reshape(KV_LORA, MLA_HEADS, MLA_NOPE + MLA_V)
    zk = jnp.zeros((KV_LORA, MLA_HEADS, HEAD_PAD - MLA_NOPE), w_ukv.dtype)
    wuk = jnp.concatenate([ukv[:, :, :MLA_NOPE], zk], axis=2).reshape(KV_LORA, MLA_HEADS * HEAD_PAD)
    wuv = ukv[:, :, MLA_NOPE:].reshape(KV_LORA, MLA_HEADS * MLA_V)
    return tuple(a.astype(BF16) for a in (w, wuq, wuqs, wuk, wuv))


def _rope_table(seq):
    pos = jnp.arange(seq, dtype=F32)
    inv_freq = ROPE_THETA ** (-jnp.arange(0, MLA_ROPE, 2, dtype=F32) / MLA_ROPE)
    ang = pos[:, None] * inv_freq[None, :]
    cos2 = jnp.concatenate([jnp.cos(ang)] * 2, axis=1)
    sin2 = jnp.concatenate([jnp.sin(ang)] * 2, axis=1)
    tail = jnp.zeros((seq, HEAD_PAD - MLA_NOPE - MLA_ROPE), F32)
    head0 = jnp.zeros((seq, MLA_NOPE), F32)
    qscale = (MLA_NOPE + MLA_ROPE) ** -0.5 * LOG2E
    cos_q = jnp.concatenate([jnp.ones((seq, MLA_NOPE), F32), cos2, tail], axis=1) * qscale
    sin_q = jnp.concatenate([head0, sin2, tail], axis=1) * qscale
    cos_k = jnp.concatenate([head0, cos2, tail], axis=1)
    sin_k = jnp.concatenate([head0, sin2, tail], axis=1)
    return jnp.concatenate([cos_q, sin_q, cos_k, sin_k], axis=1)


def _channel_dft():
    k = np.arange(FNET_GROUP_DIM)
    a = 2.0 * np.pi * np.outer(k, k) / FNET_GROUP_DIM
    m = np.concatenate([np.cos(a), -np.sin(a)], axis=1) / np.sqrt(FNET_GROUP_DIM)
    return jnp.asarray(m, BF16)


def _trunk(x, p, prm):
    b, s, _ = x.shape
    t = b * s
    x2 = x.reshape(t, D_MODEL)
    p2 = p.reshape(p.shape[0], t, PLE_DIM)
    row = lambda a: a.reshape(1, -1)

    q, k, v, ga, u, gb = _in_even(x2, row(prm["g_pre"][0]), prm["w_in_e"])
    sh = lambda a: a.reshape(b, s, a.shape[-1])
    a_out = _natten(sh(q), sh(k), sh(v), sh(ga), prm["na_table"])
    c_out = _conv_branch(sh(u), sh(gb), prm["dw_w"], row(prm["dw_b"]), row(prm["cln_g"]), row(prm["cln_b"]))
    x2 = _tail(a_out.reshape(t, NA_W), c_out.reshape(t, CONV_CH), x2, p2, 0, prm["w_out_e"],
               row(prm["g_post"][0]), prm["w_gate"][0], prm["w_ple"][0])

    wi, wuq, wuqs, wuk, wuv = prm["odd"]
    qh, kh, vh, gc, zr, zi, gd = _in_odd(x2, s, row(prm["g_pre"][1]), wi, row(prm["q_norm_g"]), row(prm["kv_norm_g"]),
                                         wuq, wuqs, wuk, wuv, _rope_table(s), prm["dft_c"])
    m_out = _mla(sh(qh), sh(kh), sh(vh), sh(gc))
    f_out = _fnet(sh(zr), sh(zi), sh(gd))
    x2 = _tail(m_out.reshape(t, MLA_HEADS * MLA_V), f_out.reshape(t, FNET_GROUPS * FNET_GROUP_DIM), x2, p2, 1,
               prm["w_out_o"], row(prm["g_post"][1]), prm["w_gate"][1], prm["w_ple"][1])
    return x2.reshape(b, s, D_MODEL)


def kernel(x_prompt, x_sample, p_prompt, p_sample, g_pre, g_post, w_ple, w_ple_gate, w_in_e, rpb, dw_w, dw_b, cln_g, cln_b, w_out_e, w_in_o, q_norm_g, kv_norm_g, w_uq, w_ukv, w_out_o):
    assert g_pre.shape[0] == 2, "one neighbourhood/conv layer followed by one latent-attention/Fourier layer"
    prm = dict(
        g_pre=g_pre, g_post=g_post, w_ple=w_ple.astype(BF16), w_gate=w_ple_gate.astype(BF16),
        w_in_e=w_in_e[0].astype(BF16), na_table=_na_bias_table(rpb[0]), dw_w=dw_w[0], dw_b=dw_b[0],
        cln_g=cln_g[0], cln_b=cln_b[0], w_out_e=w_out_e[0].astype(BF16),
        odd=_odd_weights(w_in_o[0], w_uq[0], w_ukv[0]), q_norm_g=q_norm_g[0], kv_norm_g=kv_norm_g[0],
        w_out_o=w_out_o[0].astype(BF16), dft_c=_channel_dft(),
    )
    return (_trunk(x_prompt, p_prompt, prm), _trunk(x_sample, p_sample, prm))
```

```python
import functools

import numpy as np
import jax
import jax.numpy as jnp
from jax import lax
from jax.experimental import pallas as pl
from jax.experimental.pallas import tpu as pltpu

F32 = jnp.float32
BF16 = jnp.bfloat16

D_MODEL = 1024
GRID_W = 64
NA_HEADS = 8
NA_HEAD_DIM = 64
NA_WIN_ROWS = 8
NA_WIN_COLS = 16
NA_W = NA_HEADS * NA_HEAD_DIM
CONV_CH = 512
CONV_WIDTH = 31
MLA_HEADS = 8
MLA_NOPE = 64
MLA_ROPE = 32
MLA_V = 64
Q_LORA = 256
KV_LORA = 128
ROPE_THETA = 10000.0
FNET_GROUPS = 4
FNET_GROUP_DIM = 128
PLE_DIM = 256
EPS = 1e-6

LANES = 128
HALO = 16
NEG = -1e30
LOG2E = 1.4426950408889634
VMEM_LIMIT = 56 * 1024 * 1024

TM = 512
NA_ROWS_PER_STEP = 8
NA_ROWS_PER_ITER = 4
CONV_TS = 256
CONV_RC = 32
MLA_TQ = 512
MLA_TK = 1024
MLA_SLOTS = 3
MLA_CHUNK = 64
MLA_MAX_CHAINS = 4
SUB = 8
FNET_N2 = 128


def _cparams(sem):
    return pltpu.CompilerParams(dimension_semantics=sem, vmem_limit_bytes=VMEM_LIMIT)


def _dot(a, b):
    return jnp.dot(a, b, preferred_element_type=F32)


def _dot_nt(a, b):
    return lax.dot_general(a, b, (((1,), (1,)), ((), ())), preferred_element_type=F32)


def _silu(x):
    return x * jax.nn.sigmoid(x)


def _rms(x, g):
    return x * lax.rsqrt(jnp.mean(x * x, axis=-1, keepdims=True) + EPS) * g


def _in_even_kernel(x_ref, g_ref, w_ref, q_ref, k_ref, v_ref, ga_ref, u_ref, gb_ref):
    h = _rms(x_ref[...], g_ref[...]).astype(BF16)

    def proj(c):
        return _dot(h, w_ref[:, c * NA_W:(c + 1) * NA_W])

    q_ref[...] = (proj(0) * (NA_HEAD_DIM ** -0.5)).astype(BF16)
    k_ref[...] = proj(1).astype(BF16)
    v_ref[...] = proj(2).astype(BF16)
    ga_ref[...] = _silu(proj(3)).astype(BF16)
    u_ref[...] = (proj(4) * jax.nn.sigmoid(proj(5))).astype(BF16)
    gb_ref[...] = _silu(proj(6)).astype(BF16)


def _in_even(x2, g, w):
    t = x2.shape[0]
    n_in = w.shape[1]
    tok = lambda i: (i, 0)
    fixed = lambda i: (0, 0)
    out = jax.ShapeDtypeStruct((t, NA_W), BF16)
    return pl.pallas_call(
        _in_even_kernel,
        grid=(t // TM,),
        in_specs=[pl.BlockSpec((TM, D_MODEL), tok), pl.BlockSpec((1, D_MODEL), fixed),
                  pl.BlockSpec((D_MODEL, n_in), fixed)],
        out_specs=[pl.BlockSpec((TM, NA_W), tok)] * 6,
        out_shape=[out] * 6,
        compiler_params=_cparams(("parallel",)),
        name="in_even",
    )(x2, g, w)


def _na_bias_table(rpb):
    w = np.arange(GRID_W)
    cs = np.clip(w - NA_WIN_COLS // 2, 0, GRID_W - NA_WIN_COLS)
    kc = np.arange(GRID_W)
    inside = (kc[None, :] >= cs[:, None]) & (kc[None, :] < cs[:, None] + NA_WIN_COLS)
    dc = np.clip(kc[None, :] - w[:, None] + NA_WIN_COLS - 1, 0, 2 * NA_WIN_COLS - 2)
    g = rpb[:, :, dc]
    g = jnp.where(jnp.asarray(inside)[None, None], g, NEG)
    pair = jnp.concatenate([g[:, :-1], g[:, 1:]], axis=-1)
    nd = 2 * NA_WIN_ROWS - 2
    stacked = pair.reshape(NA_HEADS // 2, 2, nd, GRID_W, 2 * GRID_W).transpose(0, 2, 1, 3, 4)
    return stacked.reshape(NA_HEADS // 2 * nd, 2 * GRID_W, 2 * GRID_W).astype(F32)


def _na_kernel(q_ref, kp_ref, kc_ref, kn_ref, vp_ref, vc_ref, vn_ref, ga_ref, tb_ref, o_ref,
               ks_ref, vs_ref, s_ref, p_ref, *, rows):
    blk = NA_ROWS_PER_STEP * GRID_W
    i = pl.program_id(1)
    ks_ref[0:blk] = kp_ref[0]
    ks_ref[blk:2 * blk] = kc_ref[0]
    ks_ref[2 * blk:3 * blk] = kn_ref[0]
    vs_ref[0:blk] = vp_ref[0]
    vs_ref[blk:2 * blk] = vc_ref[0]
    vs_ref[2 * blk:3 * blk] = vn_ref[0]
    lo = lax.broadcasted_iota(jnp.int32, (GRID_W, LANES), 1) < NA_HEAD_DIM
    n_keys = NA_WIN_ROWS * GRID_W
    n_pairs = NA_HEADS // 2
    nd = 2 * NA_WIN_ROWS - 2

    def rows_body(it, carry):
        geo = []
        for sub in range(NA_ROWS_PER_ITER):
            jr = it * NA_ROWS_PER_ITER + sub
            r = i * NA_ROWS_PER_STEP + jr
            rs = jnp.clip(r - NA_WIN_ROWS // 2, 0, rows - NA_WIN_ROWS)
            start = pl.multiple_of((rs - (i - 1) * NA_ROWS_PER_STEP) * GRID_W, GRID_W)
            geo.append((r - rs, start, pl.multiple_of(jr * GRID_W, GRID_W)))
        for sub, (cls, start, q0) in enumerate(geo):
            for pr in range(n_pairs):
                cols = slice(pr * LANES, (pr + 1) * LANES)
                qp = q_ref[0, pl.ds(q0, GRID_W), cols]
                zero = jnp.zeros_like(qp)
                qs = jnp.concatenate([jnp.where(lo, qp, zero), jnp.where(lo, zero, qp)], axis=0)
                base = pr * nd + (NA_WIN_ROWS - 1) - cls
                bias = jnp.concatenate([tb_ref[base + 2 * m] for m in range(NA_WIN_ROWS // 2)], axis=1)
                s_ref[sub * n_pairs + pr] = _dot_nt(qs, ks_ref[pl.ds(start, n_keys), cols]) + bias
        for u in range(NA_ROWS_PER_ITER * n_pairs):
            s = s_ref[u]
            p = jnp.exp(s - jnp.max(s, axis=1, keepdims=True))
            p_ref[u] = (p * (1.0 / jnp.sum(p, axis=1, keepdims=True))).astype(BF16)
        for sub, (cls, start, q0) in enumerate(geo):
            outs = []
            for pr in range(n_pairs):
                cols = slice(pr * LANES, (pr + 1) * LANES)
                o2 = _dot(p_ref[sub * n_pairs + pr], vs_ref[pl.ds(start, n_keys), cols])
                outs.append(jnp.where(lo, o2[0:GRID_W], o2[GRID_W:2 * GRID_W]))
            o = jnp.concatenate(outs, axis=1)
            ga = ga_ref[0, pl.ds(q0, GRID_W), :].astype(F32)
            o_ref[0, pl.ds(q0, GRID_W), :] = (o * ga).astype(BF16)
        return carry

    lax.fori_loop(0, NA_ROWS_PER_STEP // NA_ROWS_PER_ITER, rows_body, 0)


def _natten(q, k, v, ga, table):
    b, s, _ = q.shape
    rows = s // GRID_W
    assert rows % NA_ROWS_PER_STEP == 0 and rows >= NA_WIN_ROWS
    nblk = rows // NA_ROWS_PER_STEP
    blk = NA_ROWS_PER_STEP * GRID_W
    cur = lambda bi, i: (bi, i, 0)
    prev = lambda bi, i: (bi, jnp.maximum(i - 1, 0), 0)
    nxt = lambda bi, i: (bi, jnp.minimum(i + 1, nblk - 1), 0)
    spec = lambda im: pl.BlockSpec((1, blk, NA_W), im)
    return pl.pallas_call(
        functools.partial(_na_kernel, rows=rows),
        grid=(b, nblk),
        in_specs=[spec(cur), spec(prev), spec(cur), spec(nxt), spec(prev), spec(cur), spec(nxt), spec(cur),
                  pl.BlockSpec(table.shape, lambda bi, i: (0, 0, 0))],
        out_specs=spec(cur),
        out_shape=jax.ShapeDtypeStruct((b, s, NA_W), BF16),
        scratch_shapes=[pltpu.VMEM((3 * blk, NA_W), BF16), pltpu.VMEM((3 * blk, NA_W), BF16),
                        pltpu.VMEM((NA_ROWS_PER_ITER * NA_HEADS // 2, 2 * GRID_W, NA_WIN_ROWS * GRID_W), F32),
                        pltpu.VMEM((NA_ROWS_PER_ITER * NA_HEADS // 2, 2 * GRID_W, NA_WIN_ROWS * GRID_W), BF16)],
        compiler_params=_cparams(("parallel", "parallel")),
        name="natten",
    )(q, k, k, k, v, v, v, ga, table)


def _conv_kernel(up_ref, uc_ref, un_ref, gb_ref, w_ref, b_ref, lg_ref, lb_ref, o_ref, slab_ref, sh_ref, *, nblk):
    i = pl.program_id(1)
    ts = CONV_TS
    prev = up_ref[0].astype(F32)
    nxt = un_ref[0].astype(F32)
    slab_ref[0:HALO] = jnp.where(i > 0, prev, jnp.zeros_like(prev))
    slab_ref[HALO:HALO + ts] = uc_ref[0].astype(F32)
    slab_ref[HALO + ts:2 * HALO + ts] = jnp.where(i < nblk - 1, nxt, jnp.zeros_like(nxt))
    span = sh_ref.shape[1]
    for sft in range(1, SUB):
        sh_ref[sft - 1] = slab_ref[sft:sft + span, :]
    first = HALO - CONV_WIDTH // 2
    for rc in range(ts // CONV_RC):
        r0 = rc * CONV_RC
        acc = jnp.broadcast_to(b_ref[...], (CONV_RC, CONV_CH))
        for tap in range(CONV_WIDTH):
            sft = (first + tap) % SUB
            lo = r0 + first + tap - sft
            src = slab_ref[lo:lo + CONV_RC, :] if sft == 0 else sh_ref[sft - 1, lo:lo + CONV_RC, :]
            acc = acc + w_ref[tap:tap + 1, :] * src
        mu = jnp.mean(acc, axis=-1, keepdims=True)
        cen = acc - mu
        var = jnp.mean(cen * cen, axis=-1, keepdims=True)
        y = cen * lax.rsqrt(var + EPS) * lg_ref[...] + lb_ref[...]
        gb = gb_ref[0, r0:r0 + CONV_RC, :].astype(F32)
        o_ref[0, r0:r0 + CONV_RC, :] = (_silu(y) * gb).astype(BF16)


def _conv_branch(u, gb, dw_w, dw_b, cln_g, cln_b):
    b, s, _ = u.shape
    ts = CONV_TS
    nblk = s // ts
    per = ts // HALO
    cur = lambda bi, i: (bi, i, 0)
    fixed = lambda bi, i: (0, 0)
    halo_prev = lambda bi, i: (bi, jnp.maximum(i * per - 1, 0), 0)
    halo_next = lambda bi, i: (bi, jnp.minimum((i + 1) * per, s // HALO - 1), 0)
    vec = pl.BlockSpec((1, CONV_CH), fixed)
    return pl.pallas_call(
        functools.partial(_conv_kernel, nblk=nblk),
        grid=(b, nblk),
        in_specs=[pl.BlockSpec((1, HALO, CONV_CH), halo_prev), pl.BlockSpec((1, ts, CONV_CH), cur),
                  pl.BlockSpec((1, HALO, CONV_CH), halo_next), pl.BlockSpec((1, ts, CONV_CH), cur),
                  pl.BlockSpec((CONV_WIDTH, CONV_CH), fixed), vec, vec, vec],
        out_specs=pl.BlockSpec((1, ts, CONV_CH), cur),
        out_shape=jax.ShapeDtypeStruct((b, s, CONV_CH), BF16),
        scratch_shapes=[pltpu.VMEM((ts + 2 * HALO, CONV_CH), F32),
                        pltpu.VMEM((SUB - 1, ts + 2 * HALO - SUB, CONV_CH), F32)],
        compiler_params=_cparams(("parallel", "parallel")),
        name="conv_branch",
    )(u, u, u, gb, dw_w, dw_b, cln_g, cln_b)


def _tail_kernel(a_ref, c_ref, x_ref, p_ref, wo_ref, g_ref, wg_ref, wp_ref, o_ref):
    half = a_ref.shape[1]
    o = _dot(a_ref[...], wo_ref[0:half, :]) + _dot(c_ref[...], wo_ref[half:2 * half, :])
    x1 = x_ref[...] + _rms(o, g_ref[...])
    gate = jax.nn.sigmoid(_dot(x1.astype(BF16), wg_ref[...]))
    pe = _dot(p_ref[...].astype(BF16), wp_ref[...])
    o_ref[...] = x1 + gate * pe


def _tail(a, c, x2, p, layer, w_out, g_post, w_gate, w_ple):
    t = x2.shape[0]
    half = a.shape[1]
    tok = lambda i: (i, 0)
    fixed = lambda i: (0, 0)
    return pl.pallas_call(
        _tail_kernel,
        grid=(t // TM,),
        in_specs=[pl.BlockSpec((TM, half), tok), pl.BlockSpec((TM, half), tok), pl.BlockSpec((TM, D_MODEL), tok),
                  pl.BlockSpec((None, TM, PLE_DIM), lambda i: (layer, i, 0)),
                  pl.BlockSpec((2 * half, D_MODEL), fixed), pl.BlockSpec((1, D_MODEL), fixed),
                  pl.BlockSpec((D_MODEL, D_MODEL), fixed), pl.BlockSpec((PLE_DIM, D_MODEL), fixed)],
        out_specs=pl.BlockSpec((TM, D_MODEL), tok),
        out_shape=jax.ShapeDtypeStruct((t, D_MODEL), F32),
        compiler_params=_cparams(("parallel",)),
        name="layer_tail",
    )(a, c, x2, p, w_out, g_post, w_gate, w_ple)


_O_CQ = 0
_O_CKV = _O_CQ + Q_LORA
_O_KR = _O_CKV + KV_LORA
_O_KRS = _O_KR + LANES
_O_GC = _O_KRS + LANES
_O_F = _O_GC + MLA_HEADS * MLA_V
_O_GD = _O_F + FNET_GROUPS * FNET_GROUP_DIM
_O_END = _O_GD + FNET_GROUPS * FNET_GROUP_DIM
HEAD_PAD = LANES
MLA_VROWS = MLA_V + 16


def _in_odd_kernel(x_ref, g_ref, w_ref, qg_ref, kvg_ref, wuq_ref, wuqs_ref, wuk_ref, wuvt_ref, rope_ref, dft_ref,
                   q_ref, k_ref, vt_ref, gc_ref, zr_ref, zi_ref, gd_ref):
    h = _rms(x_ref[...], g_ref[...]).astype(BF16)

    def proj(lo, hi):
        return _dot(h, w_ref[:, lo:hi])

    cos_q = rope_ref[:, 0:LANES]
    sin_q = rope_ref[:, LANES:2 * LANES]
    cos_k = rope_ref[:, 2 * LANES:3 * LANES]
    sin_k = rope_ref[:, 3 * LANES:4 * LANES]

    cq = _rms(proj(_O_CQ, _O_CKV), qg_ref[...]).astype(BF16)
    qm = _dot(cq, wuq_ref[...])
    qs = _dot(cq, wuqs_ref[...])
    for hd in range(MLA_HEADS):
        cols = slice(hd * HEAD_PAD, (hd + 1) * HEAD_PAD)
        q_ref[:, cols] = (qm[:, cols] * cos_q + qs[:, cols] * sin_q).astype(BF16)

    ckv = _rms(proj(_O_CKV, _O_KR), kvg_ref[...]).astype(BF16)
    krot = proj(_O_KR, _O_KRS) * cos_k + proj(_O_KRS, _O_GC) * sin_k
    kn = _dot(ckv, wuk_ref[...])
    for hd in range(MLA_HEADS):
        cols = slice(hd * HEAD_PAD, (hd + 1) * HEAD_PAD)
        k_ref[:, cols] = (kn[:, cols] + krot).astype(BF16)
    vt = _dot_nt(wuvt_ref[...], ckv)
    is_one = lax.broadcasted_iota(jnp.int32, vt.shape, 0) % MLA_VROWS >= MLA_V
    vt_ref[0] = jnp.where(is_one, 1.0, vt).astype(BF16)

    gc_ref[...] = _silu(proj(_O_GC, _O_F)).astype(BF16)
    gd_ref[...] = _silu(proj(_O_GD, _O_END)).astype(BF16)
    f = proj(_O_F, _O_GD).astype(BF16)
    for grp in range(FNET_GROUPS):
        cols = slice(grp * FNET_GROUP_DIM, (grp + 1) * FNET_GROUP_DIM)
        z = _dot(f[:, cols], dft_ref[...])
        zr_ref[:, cols] = z[:, 0:FNET_GROUP_DIM].astype(BF16)
        zi_ref[:, cols] = z[:, FNET_GROUP_DIM:2 * FNET_GROUP_DIM].astype(BF16)


def _in_odd(x2, seq, g, w, qg, kvg, wuq, wuqs, wuk, wuvt, rope_tab, dft_c):
    t = x2.shape[0]
    per_seq = seq // TM
    tok = lambda i: (i, 0)
    fixed = lambda i: (0, 0)
    full = lambda a: pl.BlockSpec(a.shape, fixed)
    o1024 = jax.ShapeDtypeStruct((t, MLA_HEADS * HEAD_PAD), BF16)
    o512 = jax.ShapeDtypeStruct((t, 512), BF16)
    ovt = jax.ShapeDtypeStruct((t // seq, MLA_HEADS * MLA_VROWS, seq), BF16)
    s1024 = pl.BlockSpec((TM, MLA_HEADS * HEAD_PAD), tok)
    s512 = pl.BlockSpec((TM, 512), tok)
    svt = pl.BlockSpec((1, MLA_HEADS * MLA_VROWS, TM), lambda i: (i // per_seq, 0, i % per_seq))
    return pl.pallas_call(
        _in_odd_kernel,
        grid=(t // TM,),
        in_specs=[pl.BlockSpec((TM, D_MODEL), tok), full(g), full(w), full(qg), full(kvg), full(wuq), full(wuqs),
                  full(wuk), full(wuvt), pl.BlockSpec((TM, 4 * LANES), lambda i: (i % per_seq, 0)), full(dft_c)],
        out_specs=[s1024, s1024, svt, s512, s512, s512, s512],
        out_shape=[o1024, o1024, ovt, o512, o512, o512, o512],
        compiler_params=_cparams(("parallel",)),
        name="in_odd",
    )(x2, g, w, qg, kvg, wuq, wuqs, wuk, wuvt, rope_tab, dft_c)


def _mla_kernel(q_ref, k_ref, vt_ref, gc_ref, o_ref, m_ref, acc_ref, st_ref, p_ref):
    kt = pl.program_id(2)

    @pl.when(kt == 0)
    def _():
        m_ref[...] = jnp.full(m_ref.shape, NEG, F32)
        acc_ref[...] = jnp.zeros(acc_ref.shape, F32)

    def scores(hd):
        cols = slice(hd * HEAD_PAD, (hd + 1) * HEAD_PAD)
        return _dot_nt(k_ref[0, :, cols], q_ref[0, :, cols])

    nslot = st_ref.shape[0]
    for hd in range(nslot - 1):
        st_ref[hd] = scores(hd)
    for hd in range(MLA_HEADS):
        slot = hd % nslot
        ahead = hd + nslot - 1
        if ahead < MLA_HEADS:
            st_ref[ahead % nslot] = scores(ahead)
        rows = slice(hd * MLA_VROWS, (hd + 1) * MLA_VROWS)
        tk = st_ref.shape[1]
        parts = [st_ref[slot, r * SUB:(r + 1) * SUB, :] for r in range(MLA_MAX_CHAINS)]
        for r in range(MLA_MAX_CHAINS, tk // SUB):
            parts[r % MLA_MAX_CHAINS] = jnp.maximum(parts[r % MLA_MAX_CHAINS], st_ref[slot, r * SUB:(r + 1) * SUB, :])
        while len(parts) > 1:
            parts = [jnp.maximum(parts[2 * j], parts[2 * j + 1]) for j in range(len(parts) // 2)]
        m_prev = m_ref[hd:hd + 1, :]
        m_new = jnp.maximum(m_prev, jnp.max(parts[0], axis=0, keepdims=True))
        alpha = jnp.exp2(m_prev - m_new)
        for c in range(tk // MLA_CHUNK):
            blk = slice(c * MLA_CHUNK, (c + 1) * MLA_CHUNK)
            p_ref[slot, blk, :] = jnp.exp2(st_ref[slot, blk, :] - m_new).astype(BF16)
        acc_ref[rows, :] = alpha * acc_ref[rows, :] + _dot(vt_ref[0, rows, :], p_ref[slot])
        m_ref[hd:hd + 1, :] = m_new

    @pl.when(kt == pl.num_programs(2) - 1)
    def _():
        parts = []
        for hd in range(MLA_HEADS):
            num = acc_ref[hd * MLA_VROWS:hd * MLA_VROWS + MLA_V, :]
            den = acc_ref[hd * MLA_VROWS + MLA_V:hd * MLA_VROWS + MLA_V + 1, :]
            parts.append(num / den)
        out = jnp.concatenate(parts, axis=0).T
        o_ref[0] = (out * gc_ref[0].astype(F32)).astype(BF16)


def _mla(q, k, vt, gc):
    b, s, _ = q.shape
    tq, tk = min(MLA_TQ, s), min(MLA_TK, s)
    qmap = lambda bi, qi, ki: (bi, qi, 0)
    kmap = lambda bi, qi, ki: (bi, ki, 0)
    return pl.pallas_call(
        _mla_kernel,
        grid=(b, s // tq, s // tk),
        in_specs=[pl.BlockSpec((1, tq, MLA_HEADS * HEAD_PAD), qmap), pl.BlockSpec((1, tk, MLA_HEADS * HEAD_PAD), kmap),
                  pl.BlockSpec((1, MLA_HEADS * MLA_VROWS, tk), lambda bi, qi, ki: (bi, 0, ki)),
                  pl.BlockSpec((1, tq, MLA_HEADS * MLA_V), qmap)],
        out_specs=pl.BlockSpec((1, tq, MLA_HEADS * MLA_V), qmap),
        out_shape=jax.ShapeDtypeStruct((b, s, MLA_HEADS * MLA_V), BF16),
        scratch_shapes=[pltpu.VMEM((MLA_HEADS, tq), F32), pltpu.VMEM((MLA_HEADS * MLA_VROWS, tq), F32),
                        pltpu.VMEM((MLA_SLOTS, tk, tq), F32), pltpu.VMEM((MLA_SLOTS, tk, tq), BF16)],
        compiler_params=_cparams(("parallel", "parallel", "arbitrary")),
        name="mla_attention",
    )(q, k, vt, gc)


FNET_G = 8


def _fnet1_kernel(zr_ref, zi_ref, m1_ref, twc_ref, tws_ref, br_ref, bi_ref):
    n1 = zr_ref.shape[1]
    width = FNET_GROUPS * FNET_GROUP_DIM
    a = _dot(m1_ref[...], jnp.concatenate([zr_ref[0], zi_ref[0]], axis=0))
    ar, ai = a[0:n1], a[n1:2 * n1]
    for j in range(FNET_G):
        tc = jnp.concatenate([twc_ref[j]] * (width // LANES), axis=1)
        ts = jnp.concatenate([tws_ref[j]] * (width // LANES), axis=1)
        cols = slice(j * width, (j + 1) * width)
        br_ref[0, :, cols] = (ar[:, cols] * tc + ai[:, cols] * ts).astype(BF16)
        bi_ref[0, :, cols] = (ai[:, cols] * tc - ar[:, cols] * ts).astype(BF16)


def _fnet2_kernel(br_ref, bi_ref, gd_ref, m2_ref, o_ref):
    width = FNET_GROUPS * FNET_GROUP_DIM
    for j in range(FNET_G):
        y = _dot(m2_ref[...], jnp.concatenate([br_ref[0, j], bi_ref[0, j]], axis=0))
        cols = slice(j * width, (j + 1) * width)
        o_ref[0, :, cols] = (y * gd_ref[0, :, cols].astype(F32)).astype(BF16)


def _fnet_tables(seq):
    n2 = FNET_N2
    n1 = seq // n2
    k1 = np.arange(n1)
    a1 = 2.0 * np.pi * np.outer(k1, k1) / n1
    c1, s1 = np.cos(a1) / np.sqrt(n1), np.sin(a1) / np.sqrt(n1)
    m1 = np.block([[c1, s1], [-s1, c1]])
    k2 = np.arange(n2)
    a2 = 2.0 * np.pi * np.outer(k2, k2) / n2
    m2 = np.concatenate([np.cos(a2), np.sin(a2)], axis=1) / np.sqrt(n2)
    at = 2.0 * np.pi * np.outer(k2, k1) / seq
    twc = np.broadcast_to(np.cos(at)[:, :, None], (n2, n1, LANES))
    tws = np.broadcast_to(np.sin(at)[:, :, None], (n2, n1, LANES))
    return (jnp.asarray(m1, BF16), jnp.asarray(m2, BF16), jnp.asarray(twc, F32), jnp.asarray(tws, F32))


def _fnet(zr, zi, gd):
    b, s, width = zr.shape
    n2 = FNET_N2
    n1 = s // n2
    assert n1 % FNET_G == 0 and n2 % FNET_G == 0
    m1, m2, twc, tws = _fnet_tables(s)
    zr2 = zr.reshape(b, n1, n2 * width)
    zi2 = zi.reshape(b, n1, n2 * width)
    tcw = FNET_G * width
    dat = pl.BlockSpec((1, n1, tcw), lambda bi, j: (bi, 0, j))
    tw = pl.BlockSpec((FNET_G, n1, LANES), lambda bi, j: (j, 0, 0))
    bshape = jax.ShapeDtypeStruct((b, n1, n2 * width), BF16)
    br, bi_ = pl.pallas_call(
        _fnet1_kernel,
        grid=(b, n2 // FNET_G),
        in_specs=[dat, dat, pl.BlockSpec(m1.shape, lambda bi, j: (0, 0)), tw, tw],
        out_specs=[dat, dat],
        out_shape=[bshape, bshape],
        compiler_params=_cparams(("parallel", "parallel")),
        name="fnet_stage1",
    )(zr2, zi2, m1, twc, tws)
    br4 = br.reshape(b, n1, n2, width)
    bi4 = bi_.reshape(b, n1, n2, width)
    gd2 = gd.reshape(b, n2, n1 * width)
    bspec = pl.BlockSpec((1, FNET_G, n2, width), lambda bi, j: (bi, j, 0, 0))
    ospec = pl.BlockSpec((1, n2, tcw), lambda bi, j: (bi, 0, j))
    out = pl.pallas_call(
        _fnet2_kernel,
        grid=(b, n1 // FNET_G),
        in_specs=[bspec, bspec, ospec, pl.BlockSpec(m2.shape, lambda bi, j: (0, 0))],
        out_specs=ospec,
        out_shape=jax.ShapeDtypeStruct((b, n2, n1 * width), BF16),
        compiler_params=_cparams(("parallel", "parallel")),
        name="fnet_stage2",
    )(br4, bi4, gd2, m2)
    return out.reshape(b, s, width)


def _odd_weights(w_in, w_uq, w_ukv):
    half = MLA_ROPE // 2
    o_kr = Q_LORA + KV_LORA
    o_gc = o_kr + MLA_ROPE
    kr = w_in[:, o_kr:o_gc]
    kr_swapped = jnp.concatenate([-kr[:, half:], kr[:, :half]], axis=1)
    zeros = lambda n: jnp.zeros((D_MODEL, n), w_in.dtype)
    pad_kr = lambda a: jnp.concatenate([zeros(MLA_NOPE), a, zeros(HEAD_PAD - MLA_NOPE - MLA_ROPE)], axis=1)
    w = jnp.concatenate([w_in[:, :o_kr], pad_kr(kr), pad_kr(kr_swapped), w_in[:, o_gc:]], axis=1)
    assert w.shape[1] == _O_END

    uq = w_uq.reshape(Q_LORA, MLA_HEADS, MLA_NOPE + MLA_ROPE)
    rope = uq[:, :, MLA_NOPE:]
    rope_swapped = jnp.concatenate([-rope[:, :, half:], rope[:, :, :half]], axis=2)
    zpad = jnp.zeros((Q_LORA, MLA_HEADS, HEAD_PAD - MLA_NOPE - MLA_ROPE), w_uq.dtype)
    znope = jnp.zeros((Q_LORA, MLA_HEADS, MLA_NOPE), w_uq.dtype)
    wuq = jnp.concatenate([uq, zpad], axis=2).reshape(Q_LORA, MLA_HEADS * HEAD_PAD)
    wuqs = jnp.concatenate([znope, rope_swapped, zpad], axis=2).reshape(Q_LORA, MLA_HEADS * HEAD_PAD)

    ukv = w_ukv.reshape(KV_LORA, MLA_HEADS, MLA_NOPE + MLA_V)
    zk = jnp.zeros((KV_LORA, MLA_HEADS, HEAD_PAD - MLA_NOPE), w_ukv.dtype)
    wuk = jnp.concatenate([ukv[:, :, :MLA_NOPE], zk], axis=2).reshape(KV_LORA, MLA_HEADS * HEAD_PAD)
    zv = jnp.zeros((KV_LORA, MLA_HEADS, MLA_VROWS - MLA_V), w_ukv.dtype)
    wuvt = jnp.concatenate([ukv[:, :, MLA_NOPE:], zv], axis=2).reshape(KV_LORA, MLA_HEADS * MLA_VROWS).T
    return tuple(a.astype(BF16) for a in (w, wuq, wuqs, wuk, wuvt))


def _rope_table(seq):
    pos = jnp.arange(seq, dtype=F32)
    inv_freq = ROPE_THETA ** (-jnp.arange(0, MLA_ROPE, 2, dtype=F32) / MLA_ROPE)
    ang = pos[:, None] * inv_freq[None, :]
    cos2 = jnp.concatenate([jnp.cos(ang)] * 2, axis=1)
    sin2 = jnp.concatenate([jnp.sin(ang)] * 2, axis=1)
    tail = jnp.zeros((seq, HEAD_PAD - MLA_NOPE - MLA_ROPE), F32)
    head0 = jnp.zeros((seq, MLA_NOPE), F32)
    qscale = (MLA_NOPE + MLA_ROPE) ** -0.5 * LOG2E
    cos_q = jnp.concatenate([jnp.ones((seq, MLA_NOPE), F32), cos2, tail], axis=1) * qscale
    sin_q = jnp.concatenate([head0, sin2, tail], axis=1) * qscale
    cos_k = jnp.concatenate([head0, cos2, tail], axis=1)
    sin_k = jnp.concatenate([head0, sin2, tail], axis=1)
    return jnp.concatenate([cos_q, sin_q, cos_k, sin_k], axis=1)


def _channel_dft():
    k = np.arange(FNET_GROUP_DIM)
    a = 2.0 * np.pi * np.outer(k, k) / FNET_GROUP_DIM
    m = np.concatenate([np.cos(a), -np.sin(a)], axis=1) / np.sqrt(FNET_GROUP_DIM)
    return jnp.asarray(m, BF16)


def _trunk(x, p, prm):
    b, s, _ = x.shape
    t = b * s
    x2 = x.reshape(t, D_MODEL)
    p2 = p.reshape(p.shape[0], t, PLE_DIM)
    row = lambda a: a.reshape(1, -1)

    q, k, v, ga, u, gb = _in_even(x2, row(prm["g_pre"][0]), prm["w_in_e"])
    sh = lambda a: a.reshape(b, s, a.shape[-1])
    a_out = _natten(sh(q), sh(k), sh(v), sh(ga), prm["na_table"])
    c_out = _conv_branch(sh(u), sh(gb), prm["dw_w"], row(prm["dw_b"]), row(prm["cln_g"]), row(prm["cln_b"]))
    x2 = _tail(a_out.reshape(t, NA_W), c_out.reshape(t, CONV_CH), x2, p2, 0, prm["w_out_e"],
               row(prm["g_post"][0]), prm["w_gate"][0], prm["w_ple"][0])

    wi, wuq, wuqs, wuk, wuvt = prm["odd"]
    qh, kh, vt, gc, zr, zi, gd = _in_odd(x2, s, row(prm["g_pre"][1]), wi, row(prm["q_norm_g"]), row(prm["kv_norm_g"]),
                                         wuq, wuqs, wuk, wuvt, _rope_table(s), prm["dft_c"])
    m_out = _mla(sh(qh), sh(kh), vt, sh(gc))
    f_out = _fnet(sh(zr), sh(zi), sh(gd))
    x2 = _tail(m_out.reshape(t, MLA_HEADS * MLA_V), f_out.reshape(t, FNET_GROUPS * FNET_GROUP_DIM), x2, p2, 1,
               prm["w_out_o"], row(prm["g_post"][1]), prm["w_gate"][1], prm["w_ple"][1])
    return x2.reshape(b, s, D_MODEL)


def kernel(x_prompt, x_sample, p_prompt, p_sample, g_pre, g_post, w_ple, w_ple_gate, w_in_e, rpb, dw_w, dw_b, cln_g, cln_b, w_out_e, w_in_o, q_norm_g, kv_norm_g, w_uq, w_ukv, w_out_o):
    assert g_pre.shape[0] == 2, "one neighbourhood/conv layer followed by one latent-attention/Fourier layer"
    prm = dict(
        g_pre=g_pre, g_post=g_post, w_ple=w_ple.astype(BF16), w_gate=w_ple_gate.astype(BF16),
        w_in_e=w_in_e[0].astype(BF16), na_table=_na_bias_table(rpb[0]), dw_w=dw_w[0], dw_b=dw_b[0],
        cln_g=cln_g[0], cln_b=cln_b[0], w_out_e=w_out_e[0].astype(BF16),
        odd=_odd_weights(w_in_o[0], w_uq[0], w_ukv[0]), q_norm_g=q_norm_g[0], kv_norm_g=kv_norm_g[0],
        w_out_o=w_out_o[0].astype(BF16), dft_c=_channel_dft(),
    )
    return (_trunk(x_prompt, p_prompt, prm), _trunk(x_sample, p_sample, prm))
```

```python
import functools

import numpy as np
import jax
import jax.numpy as jnp
from jax import lax
from jax.experimental import pallas as pl
from jax.experimental.pallas import tpu as pltpu

F32 = jnp.float32
BF16 = jnp.bfloat16

D_MODEL = 1024
GRID_W = 64
NA_HEADS = 8
NA_HEAD_DIM = 64
NA_WIN_ROWS = 8
NA_WIN_COLS = 16
NA_W = NA_HEADS * NA_HEAD_DIM
CONV_CH = 512
CONV_WIDTH = 31
MLA_HEADS = 8
MLA_NOPE = 64
MLA_ROPE = 32
MLA_V = 64
Q_LORA = 256
KV_LORA = 128
ROPE_THETA = 10000.0
FNET_GROUPS = 4
FNET_GROUP_DIM = 128
PLE_DIM = 256
EPS = 1e-6

LANES = 128
HALO = 16
NEG = -1e30
LOG2E = 1.4426950408889634
VMEM_LIMIT = 56 * 1024 * 1024

TM = 1024
NA_ROWS_PER_STEP = 8
NA_ROWS_PER_ITER = 4
CONV_TS = 512
CONV_RC = 32
MLA_TQ = 512
MLA_TK = 1024
MLA_SLOTS = 3
MLA_CHUNK = 64
MLA_MAX_CHAINS = 4
SUB = 8
FNET_N2 = 128


def _cparams(sem):
    return pltpu.CompilerParams(dimension_semantics=sem, vmem_limit_bytes=VMEM_LIMIT)


def _dot(a, b):
    return jnp.dot(a, b, preferred_element_type=F32)


def _dot_nt(a, b):
    return lax.dot_general(a, b, (((1,), (1,)), ((), ())), preferred_element_type=F32)


def _silu(x):
    return x * jax.nn.sigmoid(x)


def _rms(x, g):
    return x * lax.rsqrt(jnp.mean(x * x, axis=-1, keepdims=True) + EPS) * g


def _in_even_kernel(x_ref, g_ref, w_ref, q_ref, k_ref, v_ref, ga_ref, u_ref, gb_ref):
    h = _rms(x_ref[...], g_ref[...]).astype(BF16)

    def proj(c):
        return _dot(h, w_ref[:, c * NA_W:(c + 1) * NA_W])

    q_ref[...] = (proj(0) * (NA_HEAD_DIM ** -0.5)).astype(BF16)
    k_ref[...] = proj(1).astype(BF16)
    v_ref[...] = proj(2).astype(BF16)
    ga_ref[...] = _silu(proj(3)).astype(BF16)
    u_ref[...] = (proj(4) * jax.nn.sigmoid(proj(5))).astype(BF16)
    gb_ref[...] = _silu(proj(6)).astype(BF16)


def _in_even(x2, g, w):
    t = x2.shape[0]
    n_in = w.shape[1]
    tok = lambda i: (i, 0)
    fixed = lambda i: (0, 0)
    out = jax.ShapeDtypeStruct((t, NA_W), BF16)
    return pl.pallas_call(
        _in_even_kernel,
        grid=(t // TM,),
        in_specs=[pl.BlockSpec((TM, D_MODEL), tok), pl.BlockSpec((1, D_MODEL), fixed),
                  pl.BlockSpec((D_MODEL, n_in), fixed)],
        out_specs=[pl.BlockSpec((TM, NA_W), tok)] * 6,
        out_shape=[out] * 6,
        compiler_params=_cparams(("parallel",)),
        name="in_even",
    )(x2, g, w)


def _na_bias_table(rpb):
    w = np.arange(GRID_W)
    cs = np.clip(w - NA_WIN_COLS // 2, 0, GRID_W - NA_WIN_COLS)
    kc = np.arange(GRID_W)
    inside = (kc[None, :] >= cs[:, None]) & (kc[None, :] < cs[:, None] + NA_WIN_COLS)
    dc = np.clip(kc[None, :] - w[:, None] + NA_WIN_COLS - 1, 0, 2 * NA_WIN_COLS - 2)
    g = rpb[:, :, dc]
    g = jnp.where(jnp.asarray(inside)[None, None], g, NEG)
    pair = jnp.concatenate([g[:, :-1], g[:, 1:]], axis=-1)
    nd = 2 * NA_WIN_ROWS - 2
    stacked = pair.reshape(NA_HEADS // 2, 2, nd, GRID_W, 2 * GRID_W).transpose(0, 2, 1, 3, 4)
    return stacked.reshape(NA_HEADS // 2 * nd, 2 * GRID_W, 2 * GRID_W).astype(F32)


def _na_kernel(q_ref, kp_ref, kc_ref, kn_ref, vp_ref, vc_ref, vn_ref, ga_ref, tb_ref, o_ref,
               ks_ref, vs_ref, s_ref, p_ref, *, rows):
    blk = NA_ROWS_PER_STEP * GRID_W
    i = pl.program_id(1)
    ks_ref[0:blk] = kp_ref[0]
    ks_ref[blk:2 * blk] = kc_ref[0]
    ks_ref[2 * blk:3 * blk] = kn_ref[0]
    vs_ref[0:blk] = vp_ref[0]
    vs_ref[blk:2 * blk] = vc_ref[0]
    vs_ref[2 * blk:3 * blk] = vn_ref[0]
    lo = lax.broadcasted_iota(jnp.int32, (GRID_W, LANES), 1) < NA_HEAD_DIM
    n_keys = NA_WIN_ROWS * GRID_W
    n_pairs = NA_HEADS // 2
    nd = 2 * NA_WIN_ROWS - 2

    def rows_body(it, carry):
        geo = []
        for sub in range(NA_ROWS_PER_ITER):
            jr = it * NA_ROWS_PER_ITER + sub
            r = i * NA_ROWS_PER_STEP + jr
            rs = jnp.clip(r - NA_WIN_ROWS // 2, 0, rows - NA_WIN_ROWS)
            start = pl.multiple_of((rs - (i - 1) * NA_ROWS_PER_STEP) * GRID_W, GRID_W)
            geo.append((r - rs, start, pl.multiple_of(jr * GRID_W, GRID_W)))
        for sub, (cls, start, q0) in enumerate(geo):
            for pr in range(n_pairs):
                cols = slice(pr * LANES, (pr + 1) * LANES)
                qp = q_ref[0, pl.ds(q0, GRID_W), cols]
                zero = jnp.zeros_like(qp)
                qs = jnp.concatenate([jnp.where(lo, qp, zero), jnp.where(lo, zero, qp)], axis=0)
                base = pr * nd + (NA_WIN_ROWS - 1) - cls
                bias = jnp.concatenate([tb_ref[base + 2 * m] for m in range(NA_WIN_ROWS // 2)], axis=1)
                s_ref[sub * n_pairs + pr] = _dot_nt(qs, ks_ref[pl.ds(start, n_keys), cols]) + bias
        for u in range(NA_ROWS_PER_ITER * n_pairs):
            s = s_ref[u]
            p = jnp.exp(s - jnp.max(s, axis=1, keepdims=True))
            p_ref[u] = (p * (1.0 / jnp.sum(p, axis=1, keepdims=True))).astype(BF16)
        for sub, (cls, start, q0) in enumerate(geo):
            outs = []
            for pr in range(n_pairs):
                cols = slice(pr * LANES, (pr + 1) * LANES)
                o2 = _dot(p_ref[sub * n_pairs + pr], vs_ref[pl.ds(start, n_keys), cols])
                outs.append(jnp.where(lo, o2[0:GRID_W], o2[GRID_W:2 * GRID_W]))
            o = jnp.concatenate(outs, axis=1)
            ga = ga_ref[0, pl.ds(q0, GRID_W), :].astype(F32)
            o_ref[0, pl.ds(q0, GRID_W), :] = (o * ga).astype(BF16)
        return carry

    lax.fori_loop(0, NA_ROWS_PER_STEP // NA_ROWS_PER_ITER, rows_body, 0)


def _natten(q, k, v, ga, table):
    b, s, _ = q.shape
    rows = s // GRID_W
    assert rows % NA_ROWS_PER_STEP == 0 and rows >= NA_WIN_ROWS
    nblk = rows // NA_ROWS_PER_STEP
    blk = NA_ROWS_PER_STEP * GRID_W
    cur = lambda bi, i: (bi, i, 0)
    prev = lambda bi, i: (bi, jnp.maximum(i - 1, 0), 0)
    nxt = lambda bi, i: (bi, jnp.minimum(i + 1, nblk - 1), 0)
    spec = lambda im: pl.BlockSpec((1, blk, NA_W), im)
    return pl.pallas_call(
        functools.partial(_na_kernel, rows=rows),
        grid=(b, nblk),
        in_specs=[spec(cur), spec(prev), spec(cur), spec(nxt), spec(prev), spec(cur), spec(nxt), spec(cur),
                  pl.BlockSpec(table.shape, lambda bi, i: (0, 0, 0))],
        out_specs=spec(cur),
        out_shape=jax.ShapeDtypeStruct((b, s, NA_W), BF16),
        scratch_shapes=[pltpu.VMEM((3 * blk, NA_W), BF16), pltpu.VMEM((3 * blk, NA_W), BF16),
                        pltpu.VMEM((NA_ROWS_PER_ITER * NA_HEADS // 2, 2 * GRID_W, NA_WIN_ROWS * GRID_W), F32),
                        pltpu.VMEM((NA_ROWS_PER_ITER * NA_HEADS // 2, 2 * GRID_W, NA_WIN_ROWS * GRID_W), BF16)],
        compiler_params=_cparams(("parallel", "parallel")),
        name="natten",
    )(q, k, k, k, v, v, v, ga, table)


def _conv_kernel(up_ref, uc_ref, un_ref, gb_ref, w_ref, b_ref, lg_ref, lb_ref, o_ref, slab_ref, sh_ref, *, nblk):
    i = pl.program_id(1)
    ts = CONV_TS
    prev = up_ref[0].astype(F32)
    nxt = un_ref[0].astype(F32)
    slab_ref[0:HALO] = jnp.where(i > 0, prev, jnp.zeros_like(prev))
    slab_ref[HALO:HALO + ts] = uc_ref[0].astype(F32)
    slab_ref[HALO + ts:2 * HALO + ts] = jnp.where(i < nblk - 1, nxt, jnp.zeros_like(nxt))
    span = sh_ref.shape[1]
    for sft in range(1, SUB):
        sh_ref[sft - 1] = slab_ref[sft:sft + span, :]
    first = HALO - CONV_WIDTH // 2
    for rc in range(ts // CONV_RC):
        r0 = rc * CONV_RC
        acc = jnp.broadcast_to(b_ref[...], (CONV_RC, CONV_CH))
        for tap in range(CONV_WIDTH):
            sft = (first + tap) % SUB
            lo = r0 + first + tap - sft
            src = slab_ref[lo:lo + CONV_RC, :] if sft == 0 else sh_ref[sft - 1, lo:lo + CONV_RC, :]
            acc = acc + w_ref[tap:tap + 1, :] * src
        mu = jnp.mean(acc, axis=-1, keepdims=True)
        cen = acc - mu
        var = jnp.mean(cen * cen, axis=-1, keepdims=True)
        y = cen * lax.rsqrt(var + EPS) * lg_ref[...] + lb_ref[...]
        gb = gb_ref[0, r0:r0 + CONV_RC, :].astype(F32)
        o_ref[0, r0:r0 + CONV_RC, :] = (_silu(y) * gb).astype(BF16)


def _conv_branch(u, gb, dw_w, dw_b, cln_g, cln_b):
    b, s, _ = u.shape
    ts = CONV_TS
    nblk = s // ts
    per = ts // HALO
    cur = lambda bi, i: (bi, i, 0)
    fixed = lambda bi, i: (0, 0)
    halo_prev = lambda bi, i: (bi, jnp.maximum(i * per - 1, 0), 0)
    halo_next = lambda bi, i: (bi, jnp.minimum((i + 1) * per, s // HALO - 1), 0)
    vec = pl.BlockSpec((1, CONV_CH), fixed)
    return pl.pallas_call(
        functools.partial(_conv_kernel, nblk=nblk),
        grid=(b, nblk),
        in_specs=[pl.BlockSpec((1, HALO, CONV_CH), halo_prev), pl.BlockSpec((1, ts, CONV_CH), cur),
                  pl.BlockSpec((1, HALO, CONV_CH), halo_next), pl.BlockSpec((1, ts, CONV_CH), cur),
                  pl.BlockSpec((CONV_WIDTH, CONV_CH), fixed), vec, vec, vec],
        out_specs=pl.BlockSpec((1, ts, CONV_CH), cur),
        out_shape=jax.ShapeDtypeStruct((b, s, CONV_CH), BF16),
        scratch_shapes=[pltpu.VMEM((ts + 2 * HALO, CONV_CH), F32),
                        pltpu.VMEM((SUB - 1, ts + 2 * HALO - SUB, CONV_CH), F32)],
        compiler_params=_cparams(("parallel", "parallel")),
        name="conv_branch",
    )(u, u, u, gb, dw_w, dw_b, cln_g, cln_b)


def _tail_kernel(a_ref, c_ref, x_ref, p_ref, wo_ref, g_ref, wg_ref, wp_ref, o_ref):
    half = a_ref.shape[1]
    o = _dot(a_ref[...], wo_ref[0:half, :]) + _dot(c_ref[...], wo_ref[half:2 * half, :])
    x1 = x_ref[...] + _rms(o, g_ref[...])
    gate = jax.nn.sigmoid(_dot(x1.astype(BF16), wg_ref[...]))
    pe = _dot(p_ref[...].astype(BF16), wp_ref[...])
    o_ref[...] = x1 + gate * pe


def _tail(a, c, x2, p, layer, w_out, g_post, w_gate, w_ple):
    t = x2.shape[0]
    half = a.shape[1]
    tok = lambda i: (i, 0)
    fixed = lambda i: (0, 0)
    return pl.pallas_call(
        _tail_kernel,
        grid=(t // TM,),
        in_specs=[pl.BlockSpec((TM, half), tok), pl.BlockSpec((TM, half), tok), pl.BlockSpec((TM, D_MODEL), tok),
                  pl.BlockSpec((None, TM, PLE_DIM), lambda i: (layer, i, 0)),
                  pl.BlockSpec((2 * half, D_MODEL), fixed), pl.BlockSpec((1, D_MODEL), fixed),
                  pl.BlockSpec((D_MODEL, D_MODEL), fixed), pl.BlockSpec((PLE_DIM, D_MODEL), fixed)],
        out_specs=pl.BlockSpec((TM, D_MODEL), tok),
        out_shape=jax.ShapeDtypeStruct((t, D_MODEL), F32),
        compiler_params=_cparams(("parallel",)),
        name="layer_tail",
    )(a, c, x2, p, w_out, g_post, w_gate, w_ple)


_O_CQ = 0
_O_CKV = _O_CQ + Q_LORA
_O_KR = _O_CKV + KV_LORA
_O_KRS = _O_KR + LANES
_O_GC = _O_KRS + LANES
_O_F = _O_GC + MLA_HEADS * MLA_V
_O_GD = _O_F + FNET_GROUPS * FNET_GROUP_DIM
_O_END = _O_GD + FNET_GROUPS * FNET_GROUP_DIM
HEAD_PAD = LANES
MLA_VROWS = MLA_V + 16


def _in_odd_kernel(x_ref, g_ref, w_ref, qg_ref, kvg_ref, wuq_ref, wuqs_ref, wuk_ref, wuvt_ref, rope_ref, dft_ref,
                   q_ref, k_ref, vt_ref, gc_ref, zr_ref, zi_ref, gd_ref):
    h = _rms(x_ref[...], g_ref[...]).astype(BF16)

    def proj(lo, hi):
        return _dot(h, w_ref[:, lo:hi])

    cos_q = rope_ref[:, 0:LANES]
    sin_q = rope_ref[:, LANES:2 * LANES]
    cos_k = rope_ref[:, 2 * LANES:3 * LANES]
    sin_k = rope_ref[:, 3 * LANES:4 * LANES]

    cq = _rms(proj(_O_CQ, _O_CKV), qg_ref[...]).astype(BF16)
    qm = _dot(cq, wuq_ref[...])
    qs = _dot(cq, wuqs_ref[...])
    for hd in range(MLA_HEADS):
        cols = slice(hd * HEAD_PAD, (hd + 1) * HEAD_PAD)
        q_ref[:, cols] = (qm[:, cols] * cos_q + qs[:, cols] * sin_q).astype(BF16)

    ckv = _rms(proj(_O_CKV, _O_KR), kvg_ref[...]).astype(BF16)
    krot = proj(_O_KR, _O_KRS) * cos_k + proj(_O_KRS, _O_GC) * sin_k
    kn = _dot(ckv, wuk_ref[...])
    for hd in range(MLA_HEADS):
        cols = slice(hd * HEAD_PAD, (hd + 1) * HEAD_PAD)
        k_ref[:, cols] = (kn[:, cols] + krot).astype(BF16)
    vt = _dot_nt(wuvt_ref[...], ckv)
    is_one = lax.broadcasted_iota(jnp.int32, vt.shape, 0) % MLA_VROWS >= MLA_V
    vt_ref[0] = jnp.where(is_one, 1.0, vt).astype(BF16)

    gc_ref[...] = _silu(proj(_O_GC, _O_F)).astype(BF16)
    gd_ref[...] = _silu(proj(_O_GD, _O_END)).astype(BF16)
    f = proj(_O_F, _O_GD).astype(BF16)
    for grp in range(FNET_GROUPS):
        cols = slice(grp * FNET_GROUP_DIM, (grp + 1) * FNET_GROUP_DIM)
        z = _dot(f[:, cols], dft_ref[...])
        zr_ref[:, cols] = z[:, 0:FNET_GROUP_DIM].astype(BF16)
        zi_ref[:, cols] = z[:, FNET_GROUP_DIM:2 * FNET_GROUP_DIM].astype(BF16)


def _in_odd(x2, seq, g, w, qg, kvg, wuq, wuqs, wuk, wuvt, rope_tab, dft_c):
    t = x2.shape[0]
    per_seq = seq // TM
    tok = lambda i: (i, 0)
    fixed = lambda i: (0, 0)
    full = lambda a: pl.BlockSpec(a.shape, fixed)
    o1024 = jax.ShapeDtypeStruct((t, MLA_HEADS * HEAD_PAD), BF16)
    o512 = jax.ShapeDtypeStruct((t, 512), BF16)
    ovt = jax.ShapeDtypeStruct((t // seq, MLA_HEADS * MLA_VROWS, seq), BF16)
    s1024 = pl.BlockSpec((TM, MLA_HEADS * HEAD_PAD), tok)
    s512 = pl.BlockSpec((TM, 512), tok)
    svt = pl.BlockSpec((1, MLA_HEADS * MLA_VROWS, TM), lambda i: (i // per_seq, 0, i % per_seq))
    return pl.pallas_call(
        _in_odd_kernel,
        grid=(t // TM,),
        in_specs=[pl.BlockSpec((TM, D_MODEL), tok), full(g), full(w), full(qg), full(kvg), full(wuq), full(wuqs),
                  full(wuk), full(wuvt), pl.BlockSpec((TM, 4 * LANES), lambda i: (i % per_seq, 0)), full(dft_c)],
        out_specs=[s1024, s1024, svt, s512, s512, s512, s512],
        out_shape=[o1024, o1024, ovt, o512, o512, o512, o512],
        compiler_params=_cparams(("parallel",)),
        name="in_odd",
    )(x2, g, w, qg, kvg, wuq, wuqs, wuk, wuvt, rope_tab, dft_c)


def _mla_kernel(q_ref, k_ref, vt_ref, gc_ref, o_ref, m_ref, acc_ref, st_ref, p_ref):
    kt = pl.program_id(2)

    @pl.when(kt == 0)
    def _():
        m_ref[...] = jnp.full(m_ref.shape, NEG, F32)
        acc_ref[...] = jnp.zeros(acc_ref.shape, F32)

    def scores(hd):
        cols = slice(hd * HEAD_PAD, (hd + 1) * HEAD_PAD)
        return _dot_nt(k_ref[0, :, cols], q_ref[0, :, cols])

    nslot = st_ref.shape[0]
    for hd in range(nslot - 1):
        st_ref[hd] = scores(hd)
    for hd in range(MLA_HEADS):
        slot = hd % nslot
        ahead = hd + nslot - 1
        if ahead < MLA_HEADS:
            st_ref[ahead % nslot] = scores(ahead)
        rows = slice(hd * MLA_VROWS, (hd + 1) * MLA_VROWS)
        tk = st_ref.shape[1]
        parts = [st_ref[slot, r * SUB:(r + 1) * SUB, :] for r in range(MLA_MAX_CHAINS)]
        for r in range(MLA_MAX_CHAINS, tk // SUB):
            parts[r % MLA_MAX_CHAINS] = jnp.maximum(parts[r % MLA_MAX_CHAINS], st_ref[slot, r * SUB:(r + 1) * SUB, :])
        while len(parts) > 1:
            parts = [jnp.maximum(parts[2 * j], parts[2 * j + 1]) for j in range(len(parts) // 2)]
        m_prev = m_ref[hd:hd + 1, :]
        m_new = jnp.maximum(m_prev, jnp.max(parts[0], axis=0, keepdims=True))
        alpha = jnp.exp2(m_prev - m_new)
        for c in range(tk // MLA_CHUNK):
            blk = slice(c * MLA_CHUNK, (c + 1) * MLA_CHUNK)
            p_ref[slot, blk, :] = jnp.exp2(st_ref[slot, blk, :] - m_new).astype(BF16)
        acc_ref[rows, :] = alpha * acc_ref[rows, :] + _dot(vt_ref[0, rows, :], p_ref[slot])
        m_ref[hd:hd + 1, :] = m_new

    @pl.when(kt == pl.num_programs(2) - 1)
    def _():
        parts = []
        for hd in range(MLA_HEADS):
            num = acc_ref[hd * MLA_VROWS:hd * MLA_VROWS + MLA_V, :]
            den = acc_ref[hd * MLA_VROWS + MLA_V:hd * MLA_VROWS + MLA_V + 1, :]
            parts.append(num / den)
        out = jnp.concatenate(parts, axis=0).T
        o_ref[0] = (out * gc_ref[0].astype(F32)).astype(BF16)


def _mla(q, k, vt, gc):
    b, s, _ = q.shape
    tq, tk = min(MLA_TQ, s), min(MLA_TK, s)
    qmap = lambda bi, qi, ki: (bi, qi, 0)
    kmap = lambda bi, qi, ki: (bi, ki, 0)
    return pl.pallas_call(
        _mla_kernel,
        grid=(b, s // tq, s // tk),
        in_specs=[pl.BlockSpec((1, tq, MLA_HEADS * HEAD_PAD), qmap), pl.BlockSpec((1, tk, MLA_HEADS * HEAD_PAD), kmap),
                  pl.BlockSpec((1, MLA_HEADS * MLA_VROWS, tk), lambda bi, qi, ki: (bi, 0, ki)),
                  pl.BlockSpec((1, tq, MLA_HEADS * MLA_V), qmap)],
        out_specs=pl.BlockSpec((1, tq, MLA_HEADS * MLA_V), qmap),
        out_shape=jax.ShapeDtypeStruct((b, s, MLA_HEADS * MLA_V), BF16),
        scratch_shapes=[pltpu.VMEM((MLA_HEADS, tq), F32), pltpu.VMEM((MLA_HEADS * MLA_VROWS, tq), F32),
                        pltpu.VMEM((MLA_SLOTS, tk, tq), F32), pltpu.VMEM((MLA_SLOTS, tk, tq), BF16)],
        compiler_params=_cparams(("parallel", "parallel", "arbitrary")),
        name="mla_attention",
    )(q, k, vt, gc)


FNET_COPY_ROWS = 512
FNET_UNROLL = 4


def _fnet_kernel(zr_ref, zi_ref, gd_ref, m1_ref, m2_ref, o_ref, re_ref, im_ref, y_ref):
    seq = re_ref.shape[0]
    n2 = FNET_N2
    n1 = seq // n2

    def copy_in(c, carry):
        rows = pl.ds(pl.multiple_of(c * FNET_COPY_ROWS, FNET_COPY_ROWS), FNET_COPY_ROWS)
        re_ref[rows, :] = zr_ref[0, rows, :].astype(F32)
        im_ref[rows, :] = zi_ref[0, rows, :].astype(F32)
        return carry

    lax.fori_loop(0, seq // FNET_COPY_ROWS, copy_in, 0)

    def stage1(it, carry):
        for u in range(FNET_UNROLL):
            col = it * FNET_UNROLL + u
            rows = pl.ds(col, n1, stride=n2)
            z = jnp.concatenate([re_ref[rows, :], im_ref[rows, :]], axis=0).astype(BF16)
            b = _dot(m1_ref[col], z)
            re_ref[rows, :] = b[0:n1]
            im_ref[rows, :] = b[n1:2 * n1]
        return carry

    lax.fori_loop(0, n2 // FNET_UNROLL, stage1, 0)

    def stage2(it, carry):
        for u in range(FNET_UNROLL):
            k1 = it * FNET_UNROLL + u
            rows = pl.ds(pl.multiple_of(k1 * n2, n2), n2)
            z = jnp.concatenate([re_ref[rows, :], im_ref[rows, :]], axis=0).astype(BF16)
            y_ref[pl.ds(k1, n2, stride=n1), :] = _dot(m2_ref[...], z)
        return carry

    lax.fori_loop(0, n1 // FNET_UNROLL, stage2, 0)

    def gate_out(c, carry):
        rows = pl.ds(pl.multiple_of(c * FNET_COPY_ROWS, FNET_COPY_ROWS), FNET_COPY_ROWS)
        o_ref[0, rows, :] = (y_ref[rows, :] * gd_ref[0, rows, :].astype(F32)).astype(BF16)
        return carry

    lax.fori_loop(0, seq // FNET_COPY_ROWS, gate_out, 0)


def _fnet_tables(seq):
    n2 = FNET_N2
    n1 = seq // n2
    k1 = np.arange(n1)[None, :, None]
    tok = np.arange(n1)[None, None, :] * n2 + np.arange(n2)[:, None, None]
    t = 2.0 * np.pi * ((k1 * tok) % seq) / seq
    c, s = np.cos(t) / np.sqrt(n1), np.sin(t) / np.sqrt(n1)
    m1 = np.concatenate([np.concatenate([c, s], axis=2), np.concatenate([-s, c], axis=2)], axis=1)
    k2 = np.arange(n2)
    a2 = 2.0 * np.pi * np.outer(k2, k2) / n2
    m2 = np.concatenate([np.cos(a2), np.sin(a2)], axis=1) / np.sqrt(n2)
    return jnp.asarray(m1, BF16), jnp.asarray(m2, BF16)


def _fnet(zr, zi, gd):
    b, s, width = zr.shape
    n1 = s // FNET_N2
    assert s % FNET_COPY_ROWS == 0 and n1 % FNET_UNROLL == 0 and FNET_N2 % FNET_UNROLL == 0 and n1 % SUB == 0
    m1, m2 = _fnet_tables(s)
    dat = pl.BlockSpec((1, s, FNET_GROUP_DIM), lambda bi, g: (bi, 0, g))
    return pl.pallas_call(
        _fnet_kernel,
        grid=(b, width // FNET_GROUP_DIM),
        in_specs=[dat, dat, dat, pl.BlockSpec(m1.shape, lambda bi, g: (0, 0, 0)),
                  pl.BlockSpec(m2.shape, lambda bi, g: (0, 0))],
        out_specs=dat,
        out_shape=jax.ShapeDtypeStruct((b, s, width), BF16),
        scratch_shapes=[pltpu.VMEM((s, FNET_GROUP_DIM), F32)] * 3,
        compiler_params=_cparams(("parallel", "parallel")),
        name="fnet",
    )(zr, zi, gd, m1, m2)


def _odd_weights(w_in, w_uq, w_ukv):
    half = MLA_ROPE // 2
    o_kr = Q_LORA + KV_LORA
    o_gc = o_kr + MLA_ROPE
    kr = w_in[:, o_kr:o_gc]
    kr_swapped = jnp.concatenate([-kr[:, half:], kr[:, :half]], axis=1)
    zeros = lambda n: jnp.zeros((D_MODEL, n), w_in.dtype)
    pad_kr = lambda a: jnp.concatenate([zeros(MLA_NOPE), a, zeros(HEAD_PAD - MLA_NOPE - MLA_ROPE)], axis=1)
    w = jnp.concatenate([w_in[:, :o_kr], pad_kr(kr), pad_kr(kr_swapped), w_in[:, o_gc:]], axis=1)
    assert w.shape[1] == _O_END

    uq = w_uq.reshape(Q_LORA, MLA_HEADS, MLA_NOPE + MLA_ROPE)
    rope = uq[:, :, MLA_NOPE:]
    rope_swapped = jnp.concatenate([-rope[:, :, half:], rope[:, :, :half]], axis=2)
    zpad = jnp.zeros((Q_LORA, MLA_HEADS, HEAD_PAD - MLA_NOPE - MLA_ROPE), w_uq.dtype)
    znope = jnp.zeros((Q_LORA, MLA_HEADS, MLA_NOPE), w_uq.dtype)
    wuq = jnp.concatenate([uq, zpad], axis=2).reshape(Q_LORA, MLA_HEADS * HEAD_PAD)
    wuqs = jnp.concatenate([znope, rope_swapped, zpad], axis=2).reshape(Q_LORA, MLA_HEADS * HEAD_PAD)

    ukv = w_ukv.reshape(KV_LORA, MLA_HEADS, MLA_NOPE + MLA_V)
    zk = jnp.zeros((KV_LORA, MLA_HEADS, HEAD_PAD - MLA_NOPE), w_ukv.dtype)
    wuk = jnp.concatenate([ukv[:, :, :MLA_NOPE], zk], axis=2).reshape(KV_LORA, MLA_HEADS * HEAD_PAD)
    zv = jnp.zeros((KV_LORA, MLA_HEADS, MLA_VROWS - MLA_V), w_ukv.dtype)
    wuvt = jnp.concatenate([ukv[:, :, MLA_NOPE:], zv], axis=2).reshape(KV_LORA, MLA_HEADS * MLA_VROWS).T
    return tuple(a.astype(BF16) for a in (w, wuq, wuqs, wuk, wuvt))


def _rope_table(seq):
    pos = jnp.arange(seq, dtype=F32)
    inv_freq = ROPE_THETA ** (-jnp.arange(0, MLA_ROPE, 2, dtype=F32) / MLA_ROPE)
    ang = pos[:, None] * inv_freq[None, :]
    cos2 = jnp.concatenate([jnp.cos(ang)] * 2, axis=1)
    sin2 = jnp.concatenate([jnp.sin(ang)] * 2, axis=1)
    tail = jnp.zeros((seq, HEAD_PAD - MLA_NOPE - MLA_ROPE), F32)
    head0 = jnp.zeros((seq, MLA_NOPE), F32)
    qscale = (MLA_NOPE + MLA_ROPE) ** -0.5 * LOG2E
    cos_q = jnp.concatenate([jnp.ones((seq, MLA_NOPE), F32), cos2, tail], axis=1) * qscale
    sin_q = jnp.concatenate([head0, sin2, tail], axis=1) * qscale
    cos_k = jnp.concatenate([head0, cos2, tail], axis=1)
    sin_k = jnp.concatenate([head0, sin2, tail], axis=1)
    return jnp.concatenate([cos_q, sin_q, cos_k, sin_k], axis=1)


def _channel_dft():
    k = np.arange(FNET_GROUP_DIM)
    a = 2.0 * np.pi * np.outer(k, k) / FNET_GROUP_DIM
    m = np.concatenate([np.cos(a), -np.sin(a)], axis=1) / np.sqrt(FNET_GROUP_DIM)
    return jnp.asarray(m, BF16)


def _trunk(x, p, prm):
    b, s, _ = x.shape
    t = b * s
    x2 = x.reshape(t, D_MODEL)
    p2 = p.reshape(p.shape[0], t, PLE_DIM)
    row = lambda a: a.reshape(1, -1)

    q, k, v, ga, u, gb = _in_even(x2, row(prm["g_pre"][0]), prm["w_in_e"])
    sh = lambda a: a.reshape(b, s, a.shape[-1])
    a_out = _natten(sh(q), sh(k), sh(v), sh(ga), prm["na_table"])
    c_out = _conv_branch(sh(u), sh(gb), prm["dw_w"], row(prm["dw_b"]), row(prm["cln_g"]), row(prm["cln_b"]))
    x2 = _tail(a_out.reshape(t, NA_W), c_out.reshape(t, CONV_CH), x2, p2, 0, prm["w_out_e"],
               row(prm["g_post"][0]), prm["w_gate"][0], prm["w_ple"][0])

    wi, wuq, wuqs, wuk, wuvt = prm["odd"]
    qh, kh, vt, gc, zr, zi, gd = _in_odd(x2, s, row(prm["g_pre"][1]), wi, row(prm["q_norm_g"]), row(prm["kv_norm_g"]),
                                         wuq, wuqs, wuk, wuvt, _rope_table(s), prm["dft_c"])
    m_out = _mla(sh(qh), sh(kh), vt, sh(gc))
    f_out = _fnet(sh(zr), sh(zi), sh(gd))
    x2 = _tail(m_out.reshape(t, MLA_HEADS * MLA_V), f_out.reshape(t, FNET_GROUPS * FNET_GROUP_DIM), x2, p2, 1,
               prm["w_out_o"], row(prm["g_post"][1]), prm["w_gate"][1], prm["w_ple"][1])
    return x2.reshape(b, s, D_MODEL)


def kernel(x_prompt, x_sample, p_prompt, p_sample, g_pre, g_post, w_ple, w_ple_gate, w_in_e, rpb, dw_w, dw_b, cln_g, cln_b, w_out_e, w_in_o, q_norm_g, kv_norm_g, w_uq, w_ukv, w_out_o):
    assert g_pre.shape[0] == 2, "one neighbourhood/conv layer followed by one latent-attention/Fourier layer"
    prm = dict(
        g_pre=g_pre, g_post=g_post, w_ple=w_ple.astype(BF16), w_gate=w_ple_gate.astype(BF16),
        w_in_e=w_in_e[0].astype(BF16), na_table=_na_bias_table(rpb[0]), dw_w=dw_w[0], dw_b=dw_b[0],
        cln_g=cln_g[0], cln_b=cln_b[0], w_out_e=w_out_e[0].astype(BF16),
        odd=_odd_weights(w_in_o[0], w_uq[0], w_ukv[0]), q_norm_g=q_norm_g[0], kv_norm_g=kv_norm_g[0],
        w_out_o=w_out_o[0].astype(BF16), dft_c=_channel_dft(),
    )
    return (_trunk(x_prompt, p_prompt, prm), _trunk(x_sample, p_sample, prm))
```

```python
import functools

import numpy as np
import jax
import jax.numpy as jnp
from jax import lax
from jax.experimental import pallas as pl
from jax.experimental.pallas import tpu as pltpu

F32 = jnp.float32
BF16 = jnp.bfloat16

D_MODEL = 1024
GRID_W = 64
NA_HEADS = 8
NA_HEAD_DIM = 64
NA_WIN_ROWS = 8
NA_WIN_COLS = 16
NA_W = NA_HEADS * NA_HEAD_DIM
CONV_CH = 512
CONV_WIDTH = 31
MLA_HEADS = 8
MLA_NOPE = 64
MLA_ROPE = 32
MLA_V = 64
Q_LORA = 256
KV_LORA = 128
ROPE_THETA = 10000.0
FNET_GROUPS = 4
FNET_GROUP_DIM = 128
PLE_DIM = 256
EPS = 1e-6

LANES = 128
HALO = 16
NEG = -1e30
LOG2E = 1.4426950408889634
VMEM_LIMIT = 56 * 1024 * 1024

TM = 1024
NA_ROWS_PER_STEP = 8
NA_ROWS_PER_ITER = 4
CONV_TS = 512
CONV_RC = 32
MLA_TQ = 512
MLA_TK = 1024
MLA_SLOTS = 3
MLA_FAST_MARGIN = 64.0
MLA_PROBE_KEYS = 16
MLA_CHUNK = 64
MLA_MAX_CHAINS = 4
SUB = 8
FNET_N2 = 128


def _cparams(sem):
    return pltpu.CompilerParams(dimension_semantics=sem, vmem_limit_bytes=VMEM_LIMIT)


def _dot(a, b):
    return jnp.dot(a, b, preferred_element_type=F32)


def _dot_nt(a, b):
    return lax.dot_general(a, b, (((1,), (1,)), ((), ())), preferred_element_type=F32)


def _silu(x):
    return x * jax.nn.sigmoid(x)


def _rms(x, g):
    return x * lax.rsqrt(jnp.mean(x * x, axis=-1, keepdims=True) + EPS) * g


def _in_even_kernel(x_ref, g_ref, w_ref, q_ref, k_ref, v_ref, ga_ref, u_ref, gb_ref):
    h = _rms(x_ref[...], g_ref[...]).astype(BF16)

    def proj(c):
        return _dot(h, w_ref[:, c * NA_W:(c + 1) * NA_W])

    q_ref[...] = (proj(0) * (NA_HEAD_DIM ** -0.5)).astype(BF16)
    k_ref[...] = proj(1).astype(BF16)
    v_ref[...] = proj(2).astype(BF16)
    ga_ref[...] = _silu(proj(3)).astype(BF16)
    u_ref[...] = (proj(4) * jax.nn.sigmoid(proj(5))).astype(BF16)
    gb_ref[...] = _silu(proj(6)).astype(BF16)


def _in_even(x2, g, w):
    t = x2.shape[0]
    n_in = w.shape[1]
    tok = lambda i: (i, 0)
    fixed = lambda i: (0, 0)
    out = jax.ShapeDtypeStruct((t, NA_W), BF16)
    return pl.pallas_call(
        _in_even_kernel,
        grid=(t // TM,),
        in_specs=[pl.BlockSpec((TM, D_MODEL), tok), pl.BlockSpec((1, D_MODEL), fixed),
                  pl.BlockSpec((D_MODEL, n_in), fixed)],
        out_specs=[pl.BlockSpec((TM, NA_W), tok)] * 6,
        out_shape=[out] * 6,
        compiler_params=_cparams(("parallel",)),
        name="in_even",
    )(x2, g, w)


def _na_bias_table(rpb):
    w = np.arange(GRID_W)
    cs = np.clip(w - NA_WIN_COLS // 2, 0, GRID_W - NA_WIN_COLS)
    kc = np.arange(GRID_W)
    inside = (kc[None, :] >= cs[:, None]) & (kc[None, :] < cs[:, None] + NA_WIN_COLS)
    dc = np.clip(kc[None, :] - w[:, None] + NA_WIN_COLS - 1, 0, 2 * NA_WIN_COLS - 2)
    g = rpb[:, :, dc]
    g = jnp.where(jnp.asarray(inside)[None, None], g, NEG)
    pair = jnp.concatenate([g[:, :-1], g[:, 1:]], axis=-1)
    nd = 2 * NA_WIN_ROWS - 2
    stacked = pair.reshape(NA_HEADS // 2, 2, nd, GRID_W, 2 * GRID_W).transpose(0, 2, 1, 3, 4)
    return stacked.reshape(NA_HEADS // 2 * nd, 2 * GRID_W, 2 * GRID_W).astype(F32)


def _na_kernel(q_ref, kp_ref, kc_ref, kn_ref, vp_ref, vc_ref, vn_ref, ga_ref, tb_ref, o_ref,
               ks_ref, vs_ref, s_ref, p_ref, *, rows):
    blk = NA_ROWS_PER_STEP * GRID_W
    i = pl.program_id(1)
    ks_ref[0:blk] = kp_ref[0]
    ks_ref[blk:2 * blk] = kc_ref[0]
    ks_ref[2 * blk:3 * blk] = kn_ref[0]
    vs_ref[0:blk] = vp_ref[0]
    vs_ref[blk:2 * blk] = vc_ref[0]
    vs_ref[2 * blk:3 * blk] = vn_ref[0]
    lo = lax.broadcasted_iota(jnp.int32, (GRID_W, LANES), 1) < NA_HEAD_DIM
    n_keys = NA_WIN_ROWS * GRID_W
    n_pairs = NA_HEADS // 2
    nd = 2 * NA_WIN_ROWS - 2

    def rows_body(it, carry):
        geo = []
        for sub in range(NA_ROWS_PER_ITER):
            jr = it * NA_ROWS_PER_ITER + sub
            r = i * NA_ROWS_PER_STEP + jr
            rs = jnp.clip(r - NA_WIN_ROWS // 2, 0, rows - NA_WIN_ROWS)
            start = pl.multiple_of((rs - (i - 1) * NA_ROWS_PER_STEP) * GRID_W, GRID_W)
            geo.append((r - rs, start, pl.multiple_of(jr * GRID_W, GRID_W)))
        for sub, (cls, start, q0) in enumerate(geo):
            for pr in range(n_pairs):
                cols = slice(pr * LANES, (pr + 1) * LANES)
                qp = q_ref[0, pl.ds(q0, GRID_W), cols]
                zero = jnp.zeros_like(qp)
                qs = jnp.concatenate([jnp.where(lo, qp, zero), jnp.where(lo, zero, qp)], axis=0)
                base = pr * nd + (NA_WIN_ROWS - 1) - cls
                bias = jnp.concatenate([tb_ref[base + 2 * m] for m in range(NA_WIN_ROWS // 2)], axis=1)
                s_ref[sub * n_pairs + pr] = _dot_nt(qs, ks_ref[pl.ds(start, n_keys), cols]) + bias
        for u in range(NA_ROWS_PER_ITER * n_pairs):
            s = s_ref[u]
            p = jnp.exp(s - jnp.max(s, axis=1, keepdims=True))
            p_ref[u] = (p * (1.0 / jnp.sum(p, axis=1, keepdims=True))).astype(BF16)
        for sub, (cls, start, q0) in enumerate(geo):
            outs = []
            for pr in range(n_pairs):
                cols = slice(pr * LANES, (pr + 1) * LANES)
                o2 = _dot(p_ref[sub * n_pairs + pr], vs_ref[pl.ds(start, n_keys), cols])
                outs.append(jnp.where(lo, o2[0:GRID_W], o2[GRID_W:2 * GRID_W]))
            o = jnp.concatenate(outs, axis=1)
            ga = ga_ref[0, pl.ds(q0, GRID_W), :].astype(F32)
            o_ref[0, pl.ds(q0, GRID_W), :] = (o * ga).astype(BF16)
        return carry

    lax.fori_loop(0, NA_ROWS_PER_STEP // NA_ROWS_PER_ITER, rows_body, 0)


def _natten(q, k, v, ga, table):
    b, s, _ = q.shape
    rows = s // GRID_W
    assert rows % NA_ROWS_PER_STEP == 0 and rows >= NA_WIN_ROWS
    nblk = rows // NA_ROWS_PER_STEP
    blk = NA_ROWS_PER_STEP * GRID_W
    cur = lambda bi, i: (bi, i, 0)
    prev = lambda bi, i: (bi, jnp.maximum(i - 1, 0), 0)
    nxt = lambda bi, i: (bi, jnp.minimum(i + 1, nblk - 1), 0)
    spec = lambda im: pl.BlockSpec((1, blk, NA_W), im)
    return pl.pallas_call(
        functools.partial(_na_kernel, rows=rows),
        grid=(b, nblk),
        in_specs=[spec(cur), spec(prev), spec(cur), spec(nxt), spec(prev), spec(cur), spec(nxt), spec(cur),
                  pl.BlockSpec(table.shape, lambda bi, i: (0, 0, 0))],
        out_specs=spec(cur),
        out_shape=jax.ShapeDtypeStruct((b, s, NA_W), BF16),
        scratch_shapes=[pltpu.VMEM((3 * blk, NA_W), BF16), pltpu.VMEM((3 * blk, NA_W), BF16),
                        pltpu.VMEM((NA_ROWS_PER_ITER * NA_HEADS // 2, 2 * GRID_W, NA_WIN_ROWS * GRID_W), F32),
                        pltpu.VMEM((NA_ROWS_PER_ITER * NA_HEADS // 2, 2 * GRID_W, NA_WIN_ROWS * GRID_W), BF16)],
        compiler_params=_cparams(("parallel", "parallel")),
        name="natten",
    )(q, k, k, k, v, v, v, ga, table)


def _conv_kernel(up_ref, uc_ref, un_ref, gb_ref, w_ref, b_ref, lg_ref, lb_ref, o_ref, slab_ref, sh_ref, *, nblk):
    i = pl.program_id(1)
    ts = CONV_TS
    prev = up_ref[0].astype(F32)
    nxt = un_ref[0].astype(F32)
    slab_ref[0:HALO] = jnp.where(i > 0, prev, jnp.zeros_like(prev))
    slab_ref[HALO:HALO + ts] = uc_ref[0].astype(F32)
    slab_ref[HALO + ts:2 * HALO + ts] = jnp.where(i < nblk - 1, nxt, jnp.zeros_like(nxt))
    span = sh_ref.shape[1]
    for sft in range(1, SUB):
        sh_ref[sft - 1] = slab_ref[sft:sft + span, :]
    first = HALO - CONV_WIDTH // 2
    for rc in range(ts // CONV_RC):
        r0 = rc * CONV_RC
        acc = jnp.broadcast_to(b_ref[...], (CONV_RC, CONV_CH))
        for tap in range(CONV_WIDTH):
            sft = (first + tap) % SUB
            lo = r0 + first + tap - sft
            src = slab_ref[lo:lo + CONV_RC, :] if sft == 0 else sh_ref[sft - 1, lo:lo + CONV_RC, :]
            acc = acc + w_ref[tap:tap + 1, :] * src
        mu = jnp.mean(acc, axis=-1, keepdims=True)
        cen = acc - mu
        var = jnp.mean(cen * cen, axis=-1, keepdims=True)
        y = cen * lax.rsqrt(var + EPS) * lg_ref[...] + lb_ref[...]
        gb = gb_ref[0, r0:r0 + CONV_RC, :].astype(F32)
        o_ref[0, r0:r0 + CONV_RC, :] = (_silu(y) * gb).astype(BF16)


def _conv_branch(u, gb, dw_w, dw_b, cln_g, cln_b):
    b, s, _ = u.shape
    ts = CONV_TS
    nblk = s // ts
    per = ts // HALO
    cur = lambda bi, i: (bi, i, 0)
    fixed = lambda bi, i: (0, 0)
    halo_prev = lambda bi, i: (bi, jnp.maximum(i * per - 1, 0), 0)
    halo_next = lambda bi, i: (bi, jnp.minimum((i + 1) * per, s // HALO - 1), 0)
    vec = pl.BlockSpec((1, CONV_CH), fixed)
    return pl.pallas_call(
        functools.partial(_conv_kernel, nblk=nblk),
        grid=(b, nblk),
        in_specs=[pl.BlockSpec((1, HALO, CONV_CH), halo_prev), pl.BlockSpec((1, ts, CONV_CH), cur),
                  pl.BlockSpec((1, HALO, CONV_CH), halo_next), pl.BlockSpec((1, ts, CONV_CH), cur),
                  pl.BlockSpec((CONV_WIDTH, CONV_CH), fixed), vec, vec, vec],
        out_specs=pl.BlockSpec((1, ts, CONV_CH), cur),
        out_shape=jax.ShapeDtypeStruct((b, s, CONV_CH), BF16),
        scratch_shapes=[pltpu.VMEM((ts + 2 * HALO, CONV_CH), F32),
                        pltpu.VMEM((SUB - 1, ts + 2 * HALO - SUB, CONV_CH), F32)],
        compiler_params=_cparams(("parallel", "parallel")),
        name="conv_branch",
    )(u, u, u, gb, dw_w, dw_b, cln_g, cln_b)


def _tail_kernel(a_ref, c_ref, x_ref, p_ref, wo_ref, g_ref, wg_ref, wp_ref, o_ref):
    half = a_ref.shape[1]
    o = _dot(a_ref[...], wo_ref[0:half, :]) + _dot(c_ref[...], wo_ref[half:2 * half, :])
    x1 = x_ref[...] + _rms(o, g_ref[...])
    gate = jax.nn.sigmoid(_dot(x1.astype(BF16), wg_ref[...]))
    pe = _dot(p_ref[...].astype(BF16), wp_ref[...])
    o_ref[...] = x1 + gate * pe


def _tail(a, c, x2, p, layer, w_out, g_post, w_gate, w_ple):
    t = x2.shape[0]
    half = a.shape[1]
    tok = lambda i: (i, 0)
    fixed = lambda i: (0, 0)
    return pl.pallas_call(
        _tail_kernel,
        grid=(t // TM,),
        in_specs=[pl.BlockSpec((TM, half), tok), pl.BlockSpec((TM, half), tok), pl.BlockSpec((TM, D_MODEL), tok),
                  pl.BlockSpec((None, TM, PLE_DIM), lambda i: (layer, i, 0)),
                  pl.BlockSpec((2 * half, D_MODEL), fixed), pl.BlockSpec((1, D_MODEL), fixed),
                  pl.BlockSpec((D_MODEL, D_MODEL), fixed), pl.BlockSpec((PLE_DIM, D_MODEL), fixed)],
        out_specs=pl.BlockSpec((TM, D_MODEL), tok),
        out_shape=jax.ShapeDtypeStruct((t, D_MODEL), F32),
        compiler_params=_cparams(("parallel",)),
        name="layer_tail",
    )(a, c, x2, p, w_out, g_post, w_gate, w_ple)


_O_CQ = 0
_O_CKV = _O_CQ + Q_LORA
_O_KR = _O_CKV + KV_LORA
_O_KRS = _O_KR + LANES
_O_GC = _O_KRS + LANES
_O_F = _O_GC + MLA_HEADS * MLA_V
_O_GD = _O_F + FNET_GROUPS * FNET_GROUP_DIM
_O_END = _O_GD + FNET_GROUPS * FNET_GROUP_DIM
HEAD_PAD = LANES
MLA_VROWS = MLA_V + 16


def _in_odd_kernel(x_ref, g_ref, w_ref, qg_ref, kvg_ref, wuq_ref, wuqs_ref, wuk_ref, wuvt_ref, rope_ref, dft_ref,
                   q_ref, k_ref, vt_ref, gc_ref, zr_ref, zi_ref, gd_ref):
    h = _rms(x_ref[...], g_ref[...]).astype(BF16)

    def proj(lo, hi):
        return _dot(h, w_ref[:, lo:hi])

    cos_q = rope_ref[:, 0:LANES]
    sin_q = rope_ref[:, LANES:2 * LANES]
    cos_k = rope_ref[:, 2 * LANES:3 * LANES]
    sin_k = rope_ref[:, 3 * LANES:4 * LANES]

    cq = _rms(proj(_O_CQ, _O_CKV), qg_ref[...]).astype(BF16)
    qm = _dot(cq, wuq_ref[...])
    qs = _dot(cq, wuqs_ref[...])
    for hd in range(MLA_HEADS):
        cols = slice(hd * HEAD_PAD, (hd + 1) * HEAD_PAD)
        q_ref[:, cols] = (qm[:, cols] * cos_q + qs[:, cols] * sin_q).astype(BF16)

    ckv = _rms(proj(_O_CKV, _O_KR), kvg_ref[...]).astype(BF16)
    krot = proj(_O_KR, _O_KRS) * cos_k + proj(_O_KRS, _O_GC) * sin_k
    kn = _dot(ckv, wuk_ref[...])
    for hd in range(MLA_HEADS):
        cols = slice(hd * HEAD_PAD, (hd + 1) * HEAD_PAD)
        k_ref[:, cols] = (kn[:, cols] + krot).astype(BF16)
    vt = _dot_nt(wuvt_ref[...], ckv)
    is_one = lax.broadcasted_iota(jnp.int32, vt.shape, 0) % MLA_VROWS >= MLA_V
    vt_ref[0] = jnp.where(is_one, 1.0, vt).astype(BF16)

    gc_ref[...] = _silu(proj(_O_GC, _O_F)).astype(BF16)
    gd_ref[...] = _silu(proj(_O_GD, _O_END)).astype(BF16)
    f = proj(_O_F, _O_GD).astype(BF16)
    for grp in range(FNET_GROUPS):
        cols = slice(grp * FNET_GROUP_DIM, (grp + 1) * FNET_GROUP_DIM)
        z = _dot(f[:, cols], dft_ref[...])
        zr_ref[:, cols] = z[:, 0:FNET_GROUP_DIM].astype(BF16)
        zi_ref[:, cols] = z[:, FNET_GROUP_DIM:2 * FNET_GROUP_DIM].astype(BF16)


def _in_odd(x2, seq, g, w, qg, kvg, wuq, wuqs, wuk, wuvt, rope_tab, dft_c):
    t = x2.shape[0]
    per_seq = seq // TM
    tok = lambda i: (i, 0)
    fixed = lambda i: (0, 0)
    full = lambda a: pl.BlockSpec(a.shape, fixed)
    o1024 = jax.ShapeDtypeStruct((t, MLA_HEADS * HEAD_PAD), BF16)
    o512 = jax.ShapeDtypeStruct((t, 512), BF16)
    ovt = jax.ShapeDtypeStruct((t // seq, MLA_HEADS * MLA_VROWS, seq), BF16)
    s1024 = pl.BlockSpec((TM, MLA_HEADS * HEAD_PAD), tok)
    s512 = pl.BlockSpec((TM, 512), tok)
    svt = pl.BlockSpec((1, MLA_HEADS * MLA_VROWS, TM), lambda i: (i // per_seq, 0, i % per_seq))
    return pl.pallas_call(
        _in_odd_kernel,
        grid=(t // TM,),
        in_specs=[pl.BlockSpec((TM, D_MODEL), tok), full(g), full(w), full(qg), full(kvg), full(wuq), full(wuqs),
                  full(wuk), full(wuvt), pl.BlockSpec((TM, 4 * LANES), lambda i: (i % per_seq, 0)), full(dft_c)],
        out_specs=[s1024, s1024, svt, s512, s512, s512, s512],
        out_shape=[o1024, o1024, ovt, o512, o512, o512, o512],
        compiler_params=_cparams(("parallel",)),
        name="in_odd",
    )(x2, g, w, qg, kvg, wuq, wuqs, wuk, wuvt, rope_tab, dft_c)


def _mla_kernel(q_ref, k_ref, vt_ref, gc_ref, o_ref, m_ref, acc_ref, st_ref, p_ref, tmp_ref, cm_ref, flag_ref):
    kt = pl.program_id(2)

    def scores(hd):
        cols = slice(hd * HEAD_PAD, (hd + 1) * HEAD_PAD)
        return _dot_nt(k_ref[0, :, cols], q_ref[0, :, cols])

    def renewing_step():
        nslot = st_ref.shape[0]
        for hd in range(nslot - 1):
            st_ref[hd] = scores(hd)
        for hd in range(MLA_HEADS):
            slot = hd % nslot
            ahead = hd + nslot - 1
            if ahead < MLA_HEADS:
                st_ref[ahead % nslot] = scores(ahead)
            rows = slice(hd * MLA_VROWS, (hd + 1) * MLA_VROWS)
            tk = st_ref.shape[1]
            parts = [st_ref[slot, r * SUB:(r + 1) * SUB, :] for r in range(MLA_MAX_CHAINS)]
            for r in range(MLA_MAX_CHAINS, tk // SUB):
                parts[r % MLA_MAX_CHAINS] = jnp.maximum(parts[r % MLA_MAX_CHAINS],
                                                        st_ref[slot, r * SUB:(r + 1) * SUB, :])
            while len(parts) > 1:
                parts = [jnp.maximum(parts[2 * j], parts[2 * j + 1]) for j in range(len(parts) // 2)]
            m_prev = m_ref[hd:hd + 1, :]
            m_new = jnp.maximum(m_prev, jnp.max(parts[0], axis=0, keepdims=True))
            alpha = jnp.exp2(m_prev - m_new)
            for c in range(tk // MLA_CHUNK):
                blk = slice(c * MLA_CHUNK, (c + 1) * MLA_CHUNK)
                p_ref[slot, blk, :] = jnp.exp2(st_ref[slot, blk, :] - m_new).astype(BF16)
            acc_ref[rows, :] = alpha * acc_ref[rows, :] + _dot(vt_ref[0, rows, :], p_ref[slot])
            m_ref[hd:hd + 1, :] = m_new

    def streaming_step():
        for hd in range(MLA_HEADS):
            rows = slice(hd * MLA_VROWS, (hd + 1) * MLA_VROWS)
            st = scores(hd)
            cm_ref[hd:hd + 1, :] = jnp.max(st, axis=0, keepdims=True)
            p = jnp.exp2(st - m_ref[hd:hd + 1, :]).astype(BF16)
            tmp_ref[rows, :] = _dot(vt_ref[0, rows, :], p)
        excess = jnp.max(cm_ref[...] - m_ref[...])
        flag_ref[0] = (excess > MLA_FAST_MARGIN).astype(jnp.int32)

    @pl.when(kt == 0)
    def _():
        for hd in range(MLA_HEADS):
            cols = slice(hd * HEAD_PAD, (hd + 1) * HEAD_PAD)
            probe = _dot_nt(k_ref[0, 0:MLA_PROBE_KEYS, cols], q_ref[0, :, cols])
            m_ref[hd:hd + 1, :] = jnp.max(probe, axis=0, keepdims=True)
        acc_ref[...] = jnp.zeros(acc_ref.shape, F32)

    streaming_step()

    @pl.when(flag_ref[0] == 0)
    def _():
        acc_ref[...] = acc_ref[...] + tmp_ref[...]

    @pl.when(flag_ref[0] != 0)
    def _():
        renewing_step()

    @pl.when(kt == pl.num_programs(2) - 1)
    def _():
        parts = []
        for hd in range(MLA_HEADS):
            num = acc_ref[hd * MLA_VROWS:hd * MLA_VROWS + MLA_V, :]
            den = acc_ref[hd * MLA_VROWS + MLA_V:hd * MLA_VROWS + MLA_V + 1, :]
            parts.append(num / den)
        out = jnp.concatenate(parts, axis=0).T
        o_ref[0] = (out * gc_ref[0].astype(F32)).astype(BF16)


def _mla(q, k, vt, gc):
    b, s, _ = q.shape
    tq, tk = min(MLA_TQ, s), min(MLA_TK, s)
    qmap = lambda bi, qi, ki: (bi, qi, 0)
    kmap = lambda bi, qi, ki: (bi, ki, 0)
    return pl.pallas_call(
        _mla_kernel,
        grid=(b, s // tq, s // tk),
        in_specs=[pl.BlockSpec((1, tq, MLA_HEADS * HEAD_PAD), qmap), pl.BlockSpec((1, tk, MLA_HEADS * HEAD_PAD), kmap),
                  pl.BlockSpec((1, MLA_HEADS * MLA_VROWS, tk), lambda bi, qi, ki: (bi, 0, ki)),
                  pl.BlockSpec((1, tq, MLA_HEADS * MLA_V), qmap)],
        out_specs=pl.BlockSpec((1, tq, MLA_HEADS * MLA_V), qmap),
        out_shape=jax.ShapeDtypeStruct((b, s, MLA_HEADS * MLA_V), BF16),
        scratch_shapes=[pltpu.VMEM((MLA_HEADS, tq), F32), pltpu.VMEM((MLA_HEADS * MLA_VROWS, tq), F32),
                        pltpu.VMEM((MLA_SLOTS, tk, tq), F32), pltpu.VMEM((MLA_SLOTS, tk, tq), BF16),
                        pltpu.VMEM((MLA_HEADS * MLA_VROWS, tq), F32), pltpu.VMEM((MLA_HEADS, tq), F32),
                        pltpu.SMEM((1,), jnp.int32)],
        compiler_params=_cparams(("parallel", "parallel", "arbitrary")),
        name="mla_attention",
    )(q, k, vt, gc)


FNET_COPY_ROWS = 512
FNET_UNROLL = 4


def _fnet_kernel(zr_ref, zi_ref, gd_ref, m1_ref, m2_ref, o_ref, re_ref, im_ref, y_ref):
    seq = re_ref.shape[0]
    n2 = FNET_N2
    n1 = seq // n2

    def copy_in(c, carry):
        rows = pl.ds(pl.multiple_of(c * FNET_COPY_ROWS, FNET_COPY_ROWS), FNET_COPY_ROWS)
        re_ref[rows, :] = zr_ref[0, rows, :].astype(F32)
        im_ref[rows, :] = zi_ref[0, rows, :].astype(F32)
        return carry

    lax.fori_loop(0, seq // FNET_COPY_ROWS, copy_in, 0)

    def stage1(it, carry):
        for u in range(FNET_UNROLL):
            col = it * FNET_UNROLL + u
            rows = pl.ds(col, n1, stride=n2)
            z = jnp.concatenate([re_ref[rows, :], im_ref[rows, :]], axis=0).astype(BF16)
            b = _dot(m1_ref[col], z)
            re_ref[rows, :] = b[0:n1]
            im_ref[rows, :] = b[n1:2 * n1]
        return carry

    lax.fori_loop(0, n2 // FNET_UNROLL, stage1, 0)

    def stage2(it, carry):
        for u in range(FNET_UNROLL):
            k1 = it * FNET_UNROLL + u
            rows = pl.ds(pl.multiple_of(k1 * n2, n2), n2)
            z = jnp.concatenate([re_ref[rows, :], im_ref[rows, :]], axis=0).astype(BF16)
            y_ref[pl.ds(k1, n2, stride=n1), :] = _dot(m2_ref[...], z)
        return carry

    lax.fori_loop(0, n1 // FNET_UNROLL, stage2, 0)

    def gate_out(c, carry):
        rows = pl.ds(pl.multiple_of(c * FNET_COPY_ROWS, FNET_COPY_ROWS), FNET_COPY_ROWS)
        o_ref[0, rows, :] = (y_ref[rows, :] * gd_ref[0, rows, :].astype(F32)).astype(BF16)
        return carry

    lax.fori_loop(0, seq // FNET_COPY_ROWS, gate_out, 0)


def _fnet_tables(seq):
    n2 = FNET_N2
    n1 = seq // n2
    k1 = np.arange(n1)[None, :, None]
    tok = np.arange(n1)[None, None, :] * n2 + np.arange(n2)[:, None, None]
    t = 2.0 * np.pi * ((k1 * tok) % seq) / seq
    c, s = np.cos(t) / np.sqrt(n1), np.sin(t) / np.sqrt(n1)
    m1 = np.concatenate([np.concatenate([c, s], axis=2), np.concatenate([-s, c], axis=2)], axis=1)
    k2 = np.arange(n2)
    a2 = 2.0 * np.pi * np.outer(k2, k2) / n2
    m2 = np.concatenate([np.cos(a2), np.sin(a2)], axis=1) / np.sqrt(n2)
    return jnp.asarray(m1, BF16), jnp.asarray(m2, BF16)


def _fnet(zr, zi, gd):
    b, s, width = zr.shape
    n1 = s // FNET_N2
    assert s % FNET_COPY_ROWS == 0 and n1 % FNET_UNROLL == 0 and FNET_N2 % FNET_UNROLL == 0 and n1 % SUB == 0
    m1, m2 = _fnet_tables(s)
    dat = pl.BlockSpec((1, s, FNET_GROUP_DIM), lambda bi, g: (bi, 0, g))
    return pl.pallas_call(
        _fnet_kernel,
        grid=(b, width // FNET_GROUP_DIM),
        in_specs=[dat, dat, dat, pl.BlockSpec(m1.shape, lambda bi, g: (0, 0, 0)),
                  pl.BlockSpec(m2.shape, lambda bi, g: (0, 0))],
        out_specs=dat,
        out_shape=jax.ShapeDtypeStruct((b, s, width), BF16),
        scratch_shapes=[pltpu.VMEM((s, FNET_GROUP_DIM), F32)] * 3,
        compiler_params=_cparams(("parallel", "parallel")),
        name="fnet",
    )(zr, zi, gd, m1, m2)


def _odd_weights(w_in, w_uq, w_ukv):
    half = MLA_ROPE // 2
    o_kr = Q_LORA + KV_LORA
    o_gc = o_kr + MLA_ROPE
    kr = w_in[:, o_kr:o_gc]
    kr_swapped = jnp.concatenate([-kr[:, half:], kr[:, :half]], axis=1)
    zeros = lambda n: jnp.zeros((D_MODEL, n), w_in.dtype)
    pad_kr = lambda a: jnp.concatenate([zeros(MLA_NOPE), a, zeros(HEAD_PAD - MLA_NOPE - MLA_ROPE)], axis=1)
    w = jnp.concatenate([w_in[:, :o_kr], pad_kr(kr), pad_kr(kr_swapped), w_in[:, o_gc:]], axis=1)
    assert w.shape[1] == _O_END

    uq = w_uq.reshape(Q_LORA, MLA_HEADS, MLA_NOPE + MLA_ROPE)
    rope = uq[:, :, MLA_NOPE:]
    rope_swapped = jnp.concatenate([-rope[:, :, half:], rope[:, :, :half]], axis=2)
    zpad = jnp.zeros((Q_LORA, MLA_HEADS, HEAD_PAD - MLA_NOPE - MLA_ROPE), w_uq.dtype)
    znope = jnp.zeros((Q_LORA, MLA_HEADS, MLA_NOPE), w_uq.dtype)
    wuq = jnp.concatenate([uq, zpad], axis=2).reshape(Q_LORA, MLA_HEADS * HEAD_PAD)
    wuqs = jnp.concatenate([znope, rope_swapped, zpad], axis=2).reshape(Q_LORA, MLA_HEADS * HEAD_PAD)

    ukv = w_ukv.reshape(KV_LORA, MLA_HEADS, MLA_NOPE + MLA_V)
    zk = jnp.zeros((KV_LORA, MLA_HEADS, HEAD_PAD - MLA_NOPE), w_ukv.dtype)
    wuk = jnp.concatenate([ukv[:, :, :MLA_NOPE], zk], axis=2).reshape(KV_LORA, MLA_HEADS * HEAD_PAD)
    zv = jnp.zeros((KV_LORA, MLA_HEADS, MLA_VROWS - MLA_V), w_ukv.dtype)
    wuvt = jnp.concatenate([ukv[:, :, MLA_NOPE:], zv], axis=2).reshape(KV_LORA, MLA_HEADS * MLA_VROWS).T
    return tuple(a.astype(BF16) for a in (w, wuq, wuqs, wuk, wuvt))


def _rope_table(seq):
    pos = jnp.arange(seq, dtype=F32)
    inv_freq = ROPE_THETA ** (-jnp.arange(0, MLA_ROPE, 2, dtype=F32) / MLA_ROPE)
    ang = pos[:, None] * inv_freq[None, :]
    cos2 = jnp.concatenate([jnp.cos(ang)] * 2, axis=1)
    sin2 = jnp.concatenate([jnp.sin(ang)] * 2, axis=1)
    tail = jnp.zeros((seq, HEAD_PAD - MLA_NOPE - MLA_ROPE), F32)
    head0 = jnp.zeros((seq, MLA_NOPE), F32)
    qscale = (MLA_NOPE + MLA_ROPE) ** -0.5 * LOG2E
    cos_q = jnp.concatenate([jnp.ones((seq, MLA_NOPE), F32), cos2, tail], axis=1) * qscale
    sin_q = jnp.concatenate([head0, sin2, tail], axis=1) * qscale
    cos_k = jnp.concatenate([head0, cos2, tail], axis=1)
    sin_k = jnp.concatenate([head0, sin2, tail], axis=1)
    return jnp.concatenate([cos_q, sin_q, cos_k, sin_k], axis=1)


def _channel_dft():
    k = np.arange(FNET_GROUP_DIM)
    a = 2.0 * np.pi * np.outer(k, k) / FNET_GROUP_DIM
    m = np.concatenate([np.cos(a), -np.sin(a)], axis=1) / np.sqrt(FNET_GROUP_DIM)
    return jnp.asarray(m, BF16)


def _trunk(x, p, prm):
    b, s, _ = x.shape
    t = b * s
    x2 = x.reshape(t, D_MODEL)
    p2 = p.reshape(p.shape[0], t, PLE_DIM)
    row = lambda a: a.reshape(1, -1)

    q, k, v, ga, u, gb = _in_even(x2, row(prm["g_pre"][0]), prm["w_in_e"])
    sh = lambda a: a.reshape(b, s, a.shape[-1])
    a_out = _natten(sh(q), sh(k), sh(v), sh(ga), prm["na_table"])
    c_out = _conv_branch(sh(u), sh(gb), prm["dw_w"], row(prm["dw_b"]), row(prm["cln_g"]), row(prm["cln_b"]))
    x2 = _tail(a_out.reshape(t, NA_W), c_out.reshape(t, CONV_CH), x2, p2, 0, prm["w_out_e"],
               row(prm["g_post"][0]), prm["w_gate"][0], prm["w_ple"][0])

    wi, wuq, wuqs, wuk, wuvt = prm["odd"]
    qh, kh, vt, gc, zr, zi, gd = _in_odd(x2, s, row(prm["g_pre"][1]), wi, row(prm["q_norm_g"]), row(prm["kv_norm_g"]),
                                         wuq, wuqs, wuk, wuvt, _rope_table(s), prm["dft_c"])
    m_out = _mla(sh(qh), sh(kh), vt, sh(gc))
    f_out = _fnet(sh(zr), sh(zi), sh(gd))
    x2 = _tail(m_out.reshape(t, MLA_HEADS * MLA_V), f_out.reshape(t, FNET_GROUPS * FNET_GROUP_DIM), x2, p2, 1,
               prm["w_out_o"], row(prm["g_post"][1]), prm["w_gate"][1], prm["w_ple"][1])
    return x2.reshape(b, s, D_MODEL)


def kernel(x_prompt, x_sample, p_prompt, p_sample, g_pre, g_post, w_ple, w_ple_gate, w_in_e, rpb, dw_w, dw_b, cln_g, cln_b, w_out_e, w_in_o, q_norm_g, kv_norm_g, w_uq, w_ukv, w_out_o):
    assert g_pre.shape[0] == 2, "one neighbourhood/conv layer followed by one latent-attention/Fourier layer"
    prm = dict(
        g_pre=g_pre, g_post=g_post, w_ple=w_ple.astype(BF16), w_gate=w_ple_gate.astype(BF16),
        w_in_e=w_in_e[0].astype(BF16), na_table=_na_bias_table(rpb[0]), dw_w=dw_w[0], dw_b=dw_b[0],
        cln_g=cln_g[0], cln_b=cln_b[0], w_out_e=w_out_e[0].astype(BF16),
        odd=_odd_weights(w_in_o[0], w_uq[0], w_ukv[0]), q_norm_g=q_norm_g[0], kv_norm_g=kv_norm_g[0],
        w_out_o=w_out_o[0].astype(BF16), dft_c=_channel_dft(),
    )
    return (_trunk(x_prompt, p_prompt, prm), _trunk(x_sample, p_sample, prm))
```

```python
import functools

import numpy as np
import jax
import jax.numpy as jnp
from jax import lax
from jax.experimental import pallas as pl
from jax.experimental.pallas import tpu as pltpu

F32 = jnp.float32
BF16 = jnp.bfloat16

D_MODEL = 1024
GRID_W = 64
NA_HEADS = 8
NA_HEAD_DIM = 64
NA_WIN_ROWS = 8
NA_WIN_COLS = 16
NA_W = NA_HEADS * NA_HEAD_DIM
CONV_CH = 512
CONV_WIDTH = 31
MLA_HEADS = 8
MLA_NOPE = 64
MLA_ROPE = 32
MLA_V = 64
Q_LORA = 256
KV_LORA = 128
ROPE_THETA = 10000.0
FNET_GROUPS = 4
FNET_GROUP_DIM = 128
PLE_DIM = 256
EPS = 1e-6

LANES = 128
HALO = 16
NEG = -1e30
LOG2E = 1.4426950408889634
VMEM_LIMIT = 56 * 1024 * 1024

TM = 1024
NA_ROWS_PER_STEP = 8
NA_ROWS_PER_ITER = 4
CONV_TS = 512
CONV_RC = 64
MLA_TQ = 512
MLA_TK = 2048
MLA_SLOTS = 3
MLA_FAST_MARGIN = 64.0
MLA_PROBE_KEYS = 16
MLA_CHUNK = 64
MLA_MAX_CHAINS = 4
SUB = 8
FNET_N2 = 128


def _cparams(sem):
    return pltpu.CompilerParams(dimension_semantics=sem, vmem_limit_bytes=VMEM_LIMIT)


def _dot(a, b):
    return jnp.dot(a, b, preferred_element_type=F32)


def _dot_nt(a, b):
    return lax.dot_general(a, b, (((1,), (1,)), ((), ())), preferred_element_type=F32)


def _silu(x):
    return x * jax.nn.sigmoid(x)


def _rms(x, g):
    return x * lax.rsqrt(jnp.mean(x * x, axis=-1, keepdims=True) + EPS) * g


def _in_even_kernel(x_ref, g_ref, w_ref, q_ref, k_ref, v_ref, ga_ref, u_ref, gb_ref):
    h = _rms(x_ref[...], g_ref[...]).astype(BF16)

    def proj(c):
        return _dot(h, w_ref[:, c * NA_W:(c + 1) * NA_W])

    q_ref[...] = (proj(0) * (NA_HEAD_DIM ** -0.5)).astype(BF16)
    k_ref[...] = proj(1).astype(BF16)
    v_ref[...] = proj(2).astype(BF16)
    ga_ref[...] = _silu(proj(3)).astype(BF16)
    u_ref[...] = (proj(4) * jax.nn.sigmoid(proj(5))).astype(BF16)
    gb_ref[...] = _silu(proj(6)).astype(BF16)


def _in_even(x2, g, w):
    t = x2.shape[0]
    n_in = w.shape[1]
    tok = lambda i: (i, 0)
    fixed = lambda i: (0, 0)
    out = jax.ShapeDtypeStruct((t, NA_W), BF16)
    return pl.pallas_call(
        _in_even_kernel,
        grid=(t // TM,),
        in_specs=[pl.BlockSpec((TM, D_MODEL), tok), pl.BlockSpec((1, D_MODEL), fixed),
                  pl.BlockSpec((D_MODEL, n_in), fixed)],
        out_specs=[pl.BlockSpec((TM, NA_W), tok)] * 6,
        out_shape=[out] * 6,
        compiler_params=_cparams(("parallel",)),
        name="in_even",
    )(x2, g, w)


def _na_bias_table(rpb):
    w = np.arange(GRID_W)
    cs = np.clip(w - NA_WIN_COLS // 2, 0, GRID_W - NA_WIN_COLS)
    kc = np.arange(GRID_W)
    inside = (kc[None, :] >= cs[:, None]) & (kc[None, :] < cs[:, None] + NA_WIN_COLS)
    dc = np.clip(kc[None, :] - w[:, None] + NA_WIN_COLS - 1, 0, 2 * NA_WIN_COLS - 2)
    g = rpb[:, :, dc]
    g = jnp.where(jnp.asarray(inside)[None, None], g, NEG)
    pair = jnp.concatenate([g[:, :-1], g[:, 1:]], axis=-1)
    nd = 2 * NA_WIN_ROWS - 2
    stacked = pair.reshape(NA_HEADS // 2, 2, nd, GRID_W, 2 * GRID_W).transpose(0, 2, 1, 3, 4)
    return stacked.reshape(NA_HEADS // 2 * nd, 2 * GRID_W, 2 * GRID_W).astype(F32)


def _na_kernel(q_ref, kp_ref, kc_ref, kn_ref, vp_ref, vc_ref, vn_ref, ga_ref, tb_ref, o_ref,
               ks_ref, vs_ref, s_ref, p_ref, *, rows):
    blk = NA_ROWS_PER_STEP * GRID_W
    i = pl.program_id(1)
    ks_ref[0:blk] = kp_ref[0]
    ks_ref[blk:2 * blk] = kc_ref[0]
    ks_ref[2 * blk:3 * blk] = kn_ref[0]
    vs_ref[0:blk] = vp_ref[0]
    vs_ref[blk:2 * blk] = vc_ref[0]
    vs_ref[2 * blk:3 * blk] = vn_ref[0]
    lo = lax.broadcasted_iota(jnp.int32, (GRID_W, LANES), 1) < NA_HEAD_DIM
    n_keys = NA_WIN_ROWS * GRID_W
    n_pairs = NA_HEADS // 2
    nd = 2 * NA_WIN_ROWS - 2

    def rows_body(it, carry):
        geo = []
        for sub in range(NA_ROWS_PER_ITER):
            jr = it * NA_ROWS_PER_ITER + sub
            r = i * NA_ROWS_PER_STEP + jr
            rs = jnp.clip(r - NA_WIN_ROWS // 2, 0, rows - NA_WIN_ROWS)
            start = pl.multiple_of((rs - (i - 1) * NA_ROWS_PER_STEP) * GRID_W, GRID_W)
            geo.append((r - rs, start, pl.multiple_of(jr * GRID_W, GRID_W)))
        for sub, (cls, start, q0) in enumerate(geo):
            for pr in range(n_pairs):
                cols = slice(pr * LANES, (pr + 1) * LANES)
                qp = q_ref[0, pl.ds(q0, GRID_W), cols]
                zero = jnp.zeros_like(qp)
                qs = jnp.concatenate([jnp.where(lo, qp, zero), jnp.where(lo, zero, qp)], axis=0)
                base = pr * nd + (NA_WIN_ROWS - 1) - cls
                bias = jnp.concatenate([tb_ref[base + 2 * m] for m in range(NA_WIN_ROWS // 2)], axis=1)
                s_ref[sub * n_pairs + pr] = _dot_nt(qs, ks_ref[pl.ds(start, n_keys), cols]) + bias
        for u in range(NA_ROWS_PER_ITER * n_pairs):
            s = s_ref[u]
            p = jnp.exp(s - jnp.max(s, axis=1, keepdims=True))
            p_ref[u] = (p * (1.0 / jnp.sum(p, axis=1, keepdims=True))).astype(BF16)
        for sub, (cls, start, q0) in enumerate(geo):
            outs = []
            for pr in range(n_pairs):
                cols = slice(pr * LANES, (pr + 1) * LANES)
                o2 = _dot(p_ref[sub * n_pairs + pr], vs_ref[pl.ds(start, n_keys), cols])
                outs.append(jnp.where(lo, o2[0:GRID_W], o2[GRID_W:2 * GRID_W]))
            o = jnp.concatenate(outs, axis=1)
            ga = ga_ref[0, pl.ds(q0, GRID_W), :].astype(F32)
            o_ref[0, pl.ds(q0, GRID_W), :] = (o * ga).astype(BF16)
        return carry

    lax.fori_loop(0, NA_ROWS_PER_STEP // NA_ROWS_PER_ITER, rows_body, 0)


def _natten(q, k, v, ga, table):
    b, s, _ = q.shape
    rows = s // GRID_W
    assert rows % NA_ROWS_PER_STEP == 0 and rows >= NA_WIN_ROWS
    nblk = rows // NA_ROWS_PER_STEP
    blk = NA_ROWS_PER_STEP * GRID_W
    cur = lambda bi, i: (bi, i, 0)
    prev = lambda bi, i: (bi, jnp.maximum(i - 1, 0), 0)
    nxt = lambda bi, i: (bi, jnp.minimum(i + 1, nblk - 1), 0)
    spec = lambda im: pl.BlockSpec((1, blk, NA_W), im)
    return pl.pallas_call(
        functools.partial(_na_kernel, rows=rows),
        grid=(b, nblk),
        in_specs=[spec(cur), spec(prev), spec(cur), spec(nxt), spec(prev), spec(cur), spec(nxt), spec(cur),
                  pl.BlockSpec(table.shape, lambda bi, i: (0, 0, 0))],
        out_specs=spec(cur),
        out_shape=jax.ShapeDtypeStruct((b, s, NA_W), BF16),
        scratch_shapes=[pltpu.VMEM((3 * blk, NA_W), BF16), pltpu.VMEM((3 * blk, NA_W), BF16),
                        pltpu.VMEM((NA_ROWS_PER_ITER * NA_HEADS // 2, 2 * GRID_W, NA_WIN_ROWS * GRID_W), F32),
                        pltpu.VMEM((NA_ROWS_PER_ITER * NA_HEADS // 2, 2 * GRID_W, NA_WIN_ROWS * GRID_W), BF16)],
        compiler_params=_cparams(("parallel", "parallel")),
        name="natten",
    )(q, k, k, k, v, v, v, ga, table)


def _conv_kernel(up_ref, uc_ref, un_ref, gb_ref, w_ref, b_ref, lg_ref, lb_ref, o_ref, slab_ref, sh_ref, *, nblk):
    i = pl.program_id(1)
    ts = CONV_TS
    prev = up_ref[0].astype(F32)
    nxt = un_ref[0].astype(F32)
    slab_ref[0:HALO] = jnp.where(i > 0, prev, jnp.zeros_like(prev))
    slab_ref[HALO:HALO + ts] = uc_ref[0].astype(F32)
    slab_ref[HALO + ts:2 * HALO + ts] = jnp.where(i < nblk - 1, nxt, jnp.zeros_like(nxt))
    span = sh_ref.shape[1]
    for sft in range(1, SUB):
        sh_ref[sft - 1] = slab_ref[sft:sft + span, :]
    first = HALO - CONV_WIDTH // 2
    for rc in range(ts // CONV_RC):
        r0 = rc * CONV_RC
        acc = jnp.broadcast_to(b_ref[...], (CONV_RC, CONV_CH))
        for tap in range(CONV_WIDTH):
            sft = (first + tap) % SUB
            lo = r0 + first + tap - sft
            src = slab_ref[lo:lo + CONV_RC, :] if sft == 0 else sh_ref[sft - 1, lo:lo + CONV_RC, :]
            acc = acc + w_ref[tap:tap + 1, :] * src
        mu = jnp.mean(acc, axis=-1, keepdims=True)
        cen = acc - mu
        var = jnp.mean(cen * cen, axis=-1, keepdims=True)
        y = cen * lax.rsqrt(var + EPS) * lg_ref[...] + lb_ref[...]
        gb = gb_ref[0, r0:r0 + CONV_RC, :].astype(F32)
        o_ref[0, r0:r0 + CONV_RC, :] = (_silu(y) * gb).astype(BF16)


def _conv_branch(u, gb, dw_w, dw_b, cln_g, cln_b):
    b, s, _ = u.shape
    ts = CONV_TS
    nblk = s // ts
    per = ts // HALO
    cur = lambda bi, i: (bi, i, 0)
    fixed = lambda bi, i: (0, 0)
    halo_prev = lambda bi, i: (bi, jnp.maximum(i * per - 1, 0), 0)
    halo_next = lambda bi, i: (bi, jnp.minimum((i + 1) * per, s // HALO - 1), 0)
    vec = pl.BlockSpec((1, CONV_CH), fixed)
    return pl.pallas_call(
        functools.partial(_conv_kernel, nblk=nblk),
        grid=(b, nblk),
        in_specs=[pl.BlockSpec((1, HALO, CONV_CH), halo_prev), pl.BlockSpec((1, ts, CONV_CH), cur),
                  pl.BlockSpec((1, HALO, CONV_CH), halo_next), pl.BlockSpec((1, ts, CONV_CH), cur),
                  pl.BlockSpec((CONV_WIDTH, CONV_CH), fixed), vec, vec, vec],
        out_specs=pl.BlockSpec((1, ts, CONV_CH), cur),
        out_shape=jax.ShapeDtypeStruct((b, s, CONV_CH), BF16),
        scratch_shapes=[pltpu.VMEM((ts + 2 * HALO, CONV_CH), F32),
                        pltpu.VMEM((SUB - 1, ts + 2 * HALO - SUB, CONV_CH), F32)],
        compiler_params=_cparams(("parallel", "parallel")),
        name="conv_branch",
    )(u, u, u, gb, dw_w, dw_b, cln_g, cln_b)


def _tail_kernel(a_ref, c_ref, x_ref, p_ref, wo_ref, g_ref, wg_ref, wp_ref, o_ref):
    half = a_ref.shape[1]
    o = _dot(a_ref[...], wo_ref[0:half, :]) + _dot(c_ref[...], wo_ref[half:2 * half, :])
    x1 = x_ref[...] + _rms(o, g_ref[...])
    gate = jax.nn.sigmoid(_dot(x1.astype(BF16), wg_ref[...]))
    pe = _dot(p_ref[...].astype(BF16), wp_ref[...])
    o_ref[...] = x1 + gate * pe


def _tail(a, c, x2, p, layer, w_out, g_post, w_gate, w_ple):
    t = x2.shape[0]
    half = a.shape[1]
    tok = lambda i: (i, 0)
    fixed = lambda i: (0, 0)
    return pl.pallas_call(
        _tail_kernel,
        grid=(t // TM,),
        in_specs=[pl.BlockSpec((TM, half), tok), pl.BlockSpec((TM, half), tok), pl.BlockSpec((TM, D_MODEL), tok),
                  pl.BlockSpec((None, TM, PLE_DIM), lambda i: (layer, i, 0)),
                  pl.BlockSpec((2 * half, D_MODEL), fixed), pl.BlockSpec((1, D_MODEL), fixed),
                  pl.BlockSpec((D_MODEL, D_MODEL), fixed), pl.BlockSpec((PLE_DIM, D_MODEL), fixed)],
        out_specs=pl.BlockSpec((TM, D_MODEL), tok),
        out_shape=jax.ShapeDtypeStruct((t, D_MODEL), F32),
        compiler_params=_cparams(("parallel",)),
        name="layer_tail",
    )(a, c, x2, p, w_out, g_post, w_gate, w_ple)


_O_CQ = 0
_O_CKV = _O_CQ + Q_LORA
_O_KR = _O_CKV + KV_LORA
_O_KRS = _O_KR + LANES
_O_GC = _O_KRS + LANES
_O_F = _O_GC + MLA_HEADS * MLA_V
_O_GD = _O_F + FNET_GROUPS * FNET_GROUP_DIM
_O_END = _O_GD + FNET_GROUPS * FNET_GROUP_DIM
HEAD_PAD = LANES
MLA_VROWS = MLA_V + 16


def _in_odd_kernel(x_ref, g_ref, w_ref, qg_ref, kvg_ref, wuq_ref, wuqs_ref, wuk_ref, wuvt_ref, rope_ref, dft_ref,
                   q_ref, k_ref, vt_ref, gc_ref, zr_ref, zi_ref, gd_ref):
    h = _rms(x_ref[...], g_ref[...]).astype(BF16)

    def proj(lo, hi):
        return _dot(h, w_ref[:, lo:hi])

    cos_q = rope_ref[:, 0:LANES]
    sin_q = rope_ref[:, LANES:2 * LANES]
    cos_k = rope_ref[:, 2 * LANES:3 * LANES]
    sin_k = rope_ref[:, 3 * LANES:4 * LANES]

    cq = _rms(proj(_O_CQ, _O_CKV), qg_ref[...]).astype(BF16)
    qm = _dot(cq, wuq_ref[...])
    qs = _dot(cq, wuqs_ref[...])
    for hd in range(MLA_HEADS):
        cols = slice(hd * HEAD_PAD, (hd + 1) * HEAD_PAD)
        q_ref[:, cols] = (qm[:, cols] * cos_q + qs[:, cols] * sin_q).astype(BF16)

    ckv = _rms(proj(_O_CKV, _O_KR), kvg_ref[...]).astype(BF16)
    krot = proj(_O_KR, _O_KRS) * cos_k + proj(_O_KRS, _O_GC) * sin_k
    kn = _dot(ckv, wuk_ref[...])
    for hd in range(MLA_HEADS):
        cols = slice(hd * HEAD_PAD, (hd + 1) * HEAD_PAD)
        k_ref[:, cols] = (kn[:, cols] + krot).astype(BF16)
    vt = _dot_nt(wuvt_ref[...], ckv)
    is_one = lax.broadcasted_iota(jnp.int32, vt.shape, 0) % MLA_VROWS >= MLA_V
    vt_ref[0] = jnp.where(is_one, 1.0, vt).astype(BF16)

    gc_ref[...] = _silu(proj(_O_GC, _O_F)).astype(BF16)
    gd_ref[...] = _silu(proj(_O_GD, _O_END)).astype(BF16)
    f = proj(_O_F, _O_GD).astype(BF16)
    for grp in range(FNET_GROUPS):
        cols = slice(grp * FNET_GROUP_DIM, (grp + 1) * FNET_GROUP_DIM)
        z = _dot(f[:, cols], dft_ref[...])
        zr_ref[:, cols] = z[:, 0:FNET_GROUP_DIM].astype(BF16)
        zi_ref[:, cols] = z[:, FNET_GROUP_DIM:2 * FNET_GROUP_DIM].astype(BF16)


def _in_odd(x2, seq, g, w, qg, kvg, wuq, wuqs, wuk, wuvt, rope_tab, dft_c):
    t = x2.shape[0]
    per_seq = seq // TM
    tok = lambda i: (i, 0)
    fixed = lambda i: (0, 0)
    full = lambda a: pl.BlockSpec(a.shape, fixed)
    o1024 = jax.ShapeDtypeStruct((t, MLA_HEADS * HEAD_PAD), BF16)
    o512 = jax.ShapeDtypeStruct((t, 512), BF16)
    ovt = jax.ShapeDtypeStruct((t // seq, MLA_HEADS * MLA_VROWS, seq), BF16)
    s1024 = pl.BlockSpec((TM, MLA_HEADS * HEAD_PAD), tok)
    s512 = pl.BlockSpec((TM, 512), tok)
    svt = pl.BlockSpec((1, MLA_HEADS * MLA_VROWS, TM), lambda i: (i // per_seq, 0, i % per_seq))
    return pl.pallas_call(
        _in_odd_kernel,
        grid=(t // TM,),
        in_specs=[pl.BlockSpec((TM, D_MODEL), tok), full(g), full(w), full(qg), full(kvg), full(wuq), full(wuqs),
                  full(wuk), full(wuvt), pl.BlockSpec((TM, 4 * LANES), lambda i: (i % per_seq, 0)), full(dft_c)],
        out_specs=[s1024, s1024, svt, s512, s512, s512, s512],
        out_shape=[o1024, o1024, ovt, o512, o512, o512, o512],
        compiler_params=_cparams(("parallel",)),
        name="in_odd",
    )(x2, g, w, qg, kvg, wuq, wuqs, wuk, wuvt, rope_tab, dft_c)


def _mla_kernel(q_ref, k_ref, vt_ref, gc_ref, o_ref, m_ref, acc_ref, st_ref, p_ref, tmp_ref, cm_ref, flag_ref):
    kt = pl.program_id(2)

    def scores(hd):
        cols = slice(hd * HEAD_PAD, (hd + 1) * HEAD_PAD)
        return _dot_nt(k_ref[0, :, cols], q_ref[0, :, cols])

    def renewing_step():
        nslot = st_ref.shape[0]
        for hd in range(nslot - 1):
            st_ref[hd] = scores(hd)
        for hd in range(MLA_HEADS):
            slot = hd % nslot
            ahead = hd + nslot - 1
            if ahead < MLA_HEADS:
                st_ref[ahead % nslot] = scores(ahead)
            rows = slice(hd * MLA_VROWS, (hd + 1) * MLA_VROWS)
            tk = st_ref.shape[1]
            parts = [st_ref[slot, r * SUB:(r + 1) * SUB, :] for r in range(MLA_MAX_CHAINS)]
            for r in range(MLA_MAX_CHAINS, tk // SUB):
                parts[r % MLA_MAX_CHAINS] = jnp.maximum(parts[r % MLA_MAX_CHAINS],
                                                        st_ref[slot, r * SUB:(r + 1) * SUB, :])
            while len(parts) > 1:
                parts = [jnp.maximum(parts[2 * j], parts[2 * j + 1]) for j in range(len(parts) // 2)]
            m_prev = m_ref[hd:hd + 1, :]
            m_new = jnp.maximum(m_prev, jnp.max(parts[0], axis=0, keepdims=True))
            alpha = jnp.exp2(m_prev - m_new)
            for c in range(tk // MLA_CHUNK):
                blk = slice(c * MLA_CHUNK, (c + 1) * MLA_CHUNK)
                p_ref[slot, blk, :] = jnp.exp2(st_ref[slot, blk, :] - m_new).astype(BF16)
            acc_ref[rows, :] = alpha * acc_ref[rows, :] + _dot(vt_ref[0, rows, :], p_ref[slot])
            m_ref[hd:hd + 1, :] = m_new

    def streaming_step():
        for hd in range(MLA_HEADS):
            rows = slice(hd * MLA_VROWS, (hd + 1) * MLA_VROWS)
            st = scores(hd)
            cm_ref[hd:hd + 1, :] = jnp.max(st, axis=0, keepdims=True)
            p = jnp.exp2(st - m_ref[hd:hd + 1, :]).astype(BF16)
            tmp_ref[rows, :] = _dot(vt_ref[0, rows, :], p)
        excess = jnp.max(cm_ref[...] - m_ref[...])
        flag_ref[0] = (excess > MLA_FAST_MARGIN).astype(jnp.int32)

    @pl.when(kt == 0)
    def _():
        for hd in range(MLA_HEADS):
            cols = slice(hd * HEAD_PAD, (hd + 1) * HEAD_PAD)
            probe = _dot_nt(k_ref[0, 0:MLA_PROBE_KEYS, cols], q_ref[0, :, cols])
            m_ref[hd:hd + 1, :] = jnp.max(probe, axis=0, keepdims=True)
        acc_ref[...] = jnp.zeros(acc_ref.shape, F32)

    streaming_step()

    @pl.when(flag_ref[0] == 0)
    def _():
        acc_ref[...] = acc_ref[...] + tmp_ref[...]

    @pl.when(flag_ref[0] != 0)
    def _():
        renewing_step()

    @pl.when(kt == pl.num_programs(2) - 1)
    def _():
        parts = []
        for hd in range(MLA_HEADS):
            num = acc_ref[hd * MLA_VROWS:hd * MLA_VROWS + MLA_V, :]
            den = acc_ref[hd * MLA_VROWS + MLA_V:hd * MLA_VROWS + MLA_V + 1, :]
            parts.append(num / den)
        out = jnp.concatenate(parts, axis=0).T
        o_ref[0] = (out * gc_ref[0].astype(F32)).astype(BF16)


def _mla(q, k, vt, gc):
    b, s, _ = q.shape
    tq, tk = min(MLA_TQ, s), min(MLA_TK, s)
    qmap = lambda bi, qi, ki: (bi, qi, 0)
    kmap = lambda bi, qi, ki: (bi, ki, 0)
    return pl.pallas_call(
        _mla_kernel,
        grid=(b, s // tq, s // tk),
        in_specs=[pl.BlockSpec((1, tq, MLA_HEADS * HEAD_PAD), qmap), pl.BlockSpec((1, tk, MLA_HEADS * HEAD_PAD), kmap),
                  pl.BlockSpec((1, MLA_HEADS * MLA_VROWS, tk), lambda bi, qi, ki: (bi, 0, ki)),
                  pl.BlockSpec((1, tq, MLA_HEADS * MLA_V), qmap)],
        out_specs=pl.BlockSpec((1, tq, MLA_HEADS * MLA_V), qmap),
        out_shape=jax.ShapeDtypeStruct((b, s, MLA_HEADS * MLA_V), BF16),
        scratch_shapes=[pltpu.VMEM((MLA_HEADS, tq), F32), pltpu.VMEM((MLA_HEADS * MLA_VROWS, tq), F32),
                        pltpu.VMEM((MLA_SLOTS, tk, tq), F32), pltpu.VMEM((MLA_SLOTS, tk, tq), BF16),
                        pltpu.VMEM((MLA_HEADS * MLA_VROWS, tq), F32), pltpu.VMEM((MLA_HEADS, tq), F32),
                        pltpu.SMEM((1,), jnp.int32)],
        compiler_params=_cparams(("parallel", "parallel", "arbitrary")),
        name="mla_attention",
    )(q, k, vt, gc)


FNET_COPY_ROWS = 512
FNET_UNROLL = 8


def _fnet_kernel(zr_ref, zi_ref, gd_ref, m1_ref, m2_ref, o_ref, re_ref, im_ref, y_ref):
    seq = re_ref.shape[0]
    n2 = FNET_N2
    n1 = seq // n2

    def copy_in(c, carry):
        rows = pl.ds(pl.multiple_of(c * FNET_COPY_ROWS, FNET_COPY_ROWS), FNET_COPY_ROWS)
        re_ref[rows, :] = zr_ref[0, rows, :].astype(F32)
        im_ref[rows, :] = zi_ref[0, rows, :].astype(F32)
        return carry

    lax.fori_loop(0, seq // FNET_COPY_ROWS, copy_in, 0)

    def stage1(it, carry):
        for u in range(FNET_UNROLL):
            col = it * FNET_UNROLL + u
            rows = pl.ds(col, n1, stride=n2)
            z = jnp.concatenate([re_ref[rows, :], im_ref[rows, :]], axis=0).astype(BF16)
            b = _dot(m1_ref[col], z)
            re_ref[rows, :] = b[0:n1]
            im_ref[rows, :] = b[n1:2 * n1]
        return carry

    lax.fori_loop(0, n2 // FNET_UNROLL, stage1, 0)

    def stage2(it, carry):
        for u in range(FNET_UNROLL):
            k1 = it * FNET_UNROLL + u
            rows = pl.ds(pl.multiple_of(k1 * n2, n2), n2)
            z = jnp.concatenate([re_ref[rows, :], im_ref[rows, :]], axis=0).astype(BF16)
            y_ref[pl.ds(k1, n2, stride=n1), :] = _dot(m2_ref[...], z)
        return carry

    lax.fori_loop(0, n1 // FNET_UNROLL, stage2, 0)

    def gate_out(c, carry):
        rows = pl.ds(pl.multiple_of(c * FNET_COPY_ROWS, FNET_COPY_ROWS), FNET_COPY_ROWS)
        o_ref[0, rows, :] = (y_ref[rows, :] * gd_ref[0, rows, :].astype(F32)).astype(BF16)
        return carry

    lax.fori_loop(0, seq // FNET_COPY_ROWS, gate_out, 0)


def _fnet_tables(seq):
    n2 = FNET_N2
    n1 = seq // n2
    k1 = np.arange(n1)[None, :, None]
    tok = np.arange(n1)[None, None, :] * n2 + np.arange(n2)[:, None, None]
    t = 2.0 * np.pi * ((k1 * tok) % seq) / seq
    c, s = np.cos(t) / np.sqrt(n1), np.sin(t) / np.sqrt(n1)
    m1 = np.concatenate([np.concatenate([c, s], axis=2), np.concatenate([-s, c], axis=2)], axis=1)
    k2 = np.arange(n2)
    a2 = 2.0 * np.pi * np.outer(k2, k2) / n2
    m2 = np.concatenate([np.cos(a2), np.sin(a2)], axis=1) / np.sqrt(n2)
    return jnp.asarray(m1, BF16), jnp.asarray(m2, BF16)


def _fnet(zr, zi, gd):
    b, s, width = zr.shape
    n1 = s // FNET_N2
    assert s % FNET_COPY_ROWS == 0 and n1 % FNET_UNROLL == 0 and FNET_N2 % FNET_UNROLL == 0 and n1 % SUB == 0
    m1, m2 = _fnet_tables(s)
    dat = pl.BlockSpec((1, s, FNET_GROUP_DIM), lambda bi, g: (bi, 0, g))
    return pl.pallas_call(
        _fnet_kernel,
        grid=(b, width // FNET_GROUP_DIM),
        in_specs=[dat, dat, dat, pl.BlockSpec(m1.shape, lambda bi, g: (0, 0, 0)),
                  pl.BlockSpec(m2.shape, lambda bi, g: (0, 0))],
        out_specs=dat,
        out_shape=jax.ShapeDtypeStruct((b, s, width), BF16),
        scratch_shapes=[pltpu.VMEM((s, FNET_GROUP_DIM), F32)] * 3,
        compiler_params=_cparams(("parallel", "parallel")),
        name="fnet",
    )(zr, zi, gd, m1, m2)


def _odd_weights(w_in, w_uq, w_ukv):
    half = MLA_ROPE // 2
    o_kr = Q_LORA + KV_LORA
    o_gc = o_kr + MLA_ROPE
    kr = w_in[:, o_kr:o_gc]
    kr_swapped = jnp.concatenate([-kr[:, half:], kr[:, :half]], axis=1)
    zeros = lambda n: jnp.zeros((D_MODEL, n), w_in.dtype)
    pad_kr = lambda a: jnp.concatenate([zeros(MLA_NOPE), a, zeros(HEAD_PAD - MLA_NOPE - MLA_ROPE)], axis=1)
    w = jnp.concatenate([w_in[:, :o_kr], pad_kr(kr), pad_kr(kr_swapped), w_in[:, o_gc:]], axis=1)
    assert w.shape[1] == _O_END

    uq = w_uq.reshape(Q_LORA, MLA_HEADS, MLA_NOPE + MLA_ROPE)
    rope = uq[:, :, MLA_NOPE:]
    rope_swapped = jnp.concatenate([-rope[:, :, half:], rope[:, :, :half]], axis=2)
    zpad = jnp.zeros((Q_LORA, MLA_HEADS, HEAD_PAD - MLA_NOPE - MLA_ROPE), w_uq.dtype)
    znope = jnp.zeros((Q_LORA, MLA_HEADS, MLA_NOPE), w_uq.dtype)
    wuq = jnp.concatenate([uq, zpad], axis=2).reshape(Q_LORA, MLA_HEADS * HEAD_PAD)
    wuqs = jnp.concatenate([znope, rope_swapped, zpad], axis=2).reshape(Q_LORA, MLA_HEADS * HEAD_PAD)

    ukv = w_ukv.reshape(KV_LORA, MLA_HEADS, MLA_NOPE + MLA_V)
    zk = jnp.zeros((KV_LORA, MLA_HEADS, HEAD_PAD - MLA_NOPE), w_ukv.dtype)
    wuk = jnp.concatenate([ukv[:, :, :MLA_NOPE], zk], axis=2).reshape(KV_LORA, MLA_HEADS * HEAD_PAD)
    zv = jnp.zeros((KV_LORA, MLA_HEADS, MLA_VROWS - MLA_V), w_ukv.dtype)
    wuvt = jnp.concatenate([ukv[:, :, MLA_NOPE:], zv], axis=2).reshape(KV_LORA, MLA_HEADS * MLA_VROWS).T
    return tuple(a.astype(BF16) for a in (w, wuq, wuqs, wuk, wuvt))


def _rope_table(seq):
    pos = jnp.arange(seq, dtype=F32)
    inv_freq = ROPE_THETA ** (-jnp.arange(0, MLA_ROPE, 2, dtype=F32) / MLA_ROPE)
    ang = pos[:, None] * inv_freq[None, :]
    cos2 = jnp.concatenate([jnp.cos(ang)] * 2, axis=1)
    sin2 = jnp.concatenate([jnp.sin(ang)] * 2, axis=1)
    tail = jnp.zeros((seq, HEAD_PAD - MLA_NOPE - MLA_ROPE), F32)
    head0 = jnp.zeros((seq, MLA_NOPE), F32)
    qscale = (MLA_NOPE + MLA_ROPE) ** -0.5 * LOG2E
    cos_q = jnp.concatenate([jnp.ones((seq, MLA_NOPE), F32), cos2, tail], axis=1) * qscale
    sin_q = jnp.concatenate([head0, sin2, tail], axis=1) * qscale
    cos_k = jnp.concatenate([head0, cos2, tail], axis=1)
    sin_k = jnp.concatenate([head0, sin2, tail], axis=1)
    return jnp.concatenate([cos_q, sin_q, cos_k, sin_k], axis=1)


def _channel_dft():
    k = np.arange(FNET_GROUP_DIM)
    a = 2.0 * np.pi * np.outer(k, k) / FNET_GROUP_DIM
    m = np.concatenate([np.cos(a), -np.sin(a)], axis=1) / np.sqrt(FNET_GROUP_DIM)
    return jnp.asarray(m, BF16)


def _trunk(x, p, prm):
    b, s, _ = x.shape
    t = b * s
    x2 = x.reshape(t, D_MODEL)
    p2 = p.reshape(p.shape[0], t, PLE_DIM)
    row = lambda a: a.reshape(1, -1)

    q, k, v, ga, u, gb = _in_even(x2, row(prm["g_pre"][0]), prm["w_in_e"])
    sh = lambda a: a.reshape(b, s, a.shape[-1])
    a_out = _natten(sh(q), sh(k), sh(v), sh(ga), prm["na_table"])
    c_out = _conv_branch(sh(u), sh(gb), prm["dw_w"], row(prm["dw_b"]), row(prm["cln_g"]), row(prm["cln_b"]))
    x2 = _tail(a_out.reshape(t, NA_W), c_out.reshape(t, CONV_CH), x2, p2, 0, prm["w_out_e"],
               row(prm["g_post"][0]), prm["w_gate"][0], prm["w_ple"][0])

    wi, wuq, wuqs, wuk, wuvt = prm["odd"]
    qh, kh, vt, gc, zr, zi, gd = _in_odd(x2, s, row(prm["g_pre"][1]), wi, row(prm["q_norm_g"]), row(prm["kv_norm_g"]),
                                         wuq, wuqs, wuk, wuvt, _rope_table(s), prm["dft_c"])
    m_out = _mla(sh(qh), sh(kh), vt, sh(gc))
    f_out = _fnet(sh(zr), sh(zi), sh(gd))
    x2 = _tail(m_out.reshape(t, MLA_HEADS * MLA_V), f_out.reshape(t, FNET_GROUPS * FNET_GROUP_DIM), x2, p2, 1,
               prm["w_out_o"], row(prm["g_post"][1]), prm["w_gate"][1], prm["w_ple"][1])
    return x2.reshape(b, s, D_MODEL)


def kernel(x_prompt, x_sample, p_prompt, p_sample, g_pre, g_post, w_ple, w_ple_gate, w_in_e, rpb, dw_w, dw_b, cln_g, cln_b, w_out_e, w_in_o, q_norm_g, kv_norm_g, w_uq, w_ukv, w_out_o):
    assert g_pre.shape[0] == 2, "one neighbourhood/conv layer followed by one latent-attention/Fourier layer"
    prm = dict(
        g_pre=g_pre, g_post=g_post, w_ple=w_ple.astype(BF16), w_gate=w_ple_gate.astype(BF16),
        w_in_e=w_in_e[0].astype(BF16), na_table=_na_bias_table(rpb[0]), dw_w=dw_w[0], dw_b=dw_b[0],
        cln_g=cln_g[0], cln_b=cln_b[0], w_out_e=w_out_e[0].astype(BF16),
        odd=_odd_weights(w_in_o[0], w_uq[0], w_ukv[0]), q_norm_g=q_norm_g[0], kv_norm_g=kv_norm_g[0],
        w_out_o=w_out_o[0].astype(BF16), dft_c=_channel_dft(),
    )
    return (_trunk(x_prompt, p_prompt, prm), _trunk(x_sample, p_sample, prm))
```

```python
import functools

import numpy as np
import jax
import jax.numpy as jnp
from jax import lax
from jax.experimental import pallas as pl
from jax.experimental.pallas import tpu as pltpu

F32 = jnp.float32
BF16 = jnp.bfloat16

D_MODEL = 1024
GRID_W = 64
NA_HEADS = 8
NA_HEAD_DIM = 64
NA_WIN_ROWS = 8
NA_WIN_COLS = 16
NA_W = NA_HEADS * NA_HEAD_DIM
CONV_CH = 512
CONV_WIDTH = 31
MLA_HEADS = 8
MLA_NOPE = 64
MLA_ROPE = 32
MLA_V = 64
Q_LORA = 256
KV_LORA = 128
ROPE_THETA = 10000.0
FNET_GROUPS = 4
FNET_GROUP_DIM = 128
PLE_DIM = 256
EPS = 1e-6

LANES = 128
HALO = 16
NEG = -1e30
LOG2E = 1.4426950408889634
VMEM_LIMIT = 56 * 1024 * 1024

TM = 1024
NA_ROWS_PER_STEP = 8
NA_ROWS_PER_ITER = 8
CONV_TS = 512
CONV_RC = 64
MLA_TQ = 512
MLA_TK = 2048
MLA_SLOTS = 3
MLA_FAST_MARGIN = 64.0
MLA_PROBE_KEYS = 16
MLA_CHUNK = 64
MLA_MAX_CHAINS = 4
SUB = 8
FNET_N2 = 128


def _cparams(sem):
    return pltpu.CompilerParams(dimension_semantics=sem, vmem_limit_bytes=VMEM_LIMIT)


def _dot(a, b):
    return jnp.dot(a, b, preferred_element_type=F32)


def _dot_nt(a, b):
    return lax.dot_general(a, b, (((1,), (1,)), ((), ())), preferred_element_type=F32)


def _silu(x):
    return x * jax.nn.sigmoid(x)


def _rms(x, g):
    return x * lax.rsqrt(jnp.mean(x * x, axis=-1, keepdims=True) + EPS) * g


def _in_even_kernel(x_ref, g_ref, w_ref, q_ref, k_ref, v_ref, ga_ref, u_ref, gb_ref):
    h = _rms(x_ref[...], g_ref[...]).astype(BF16)

    def proj(c):
        return _dot(h, w_ref[:, c * NA_W:(c + 1) * NA_W])

    q_ref[...] = (proj(0) * (NA_HEAD_DIM ** -0.5)).astype(BF16)
    k_ref[...] = proj(1).astype(BF16)
    v_ref[...] = proj(2).astype(BF16)
    ga_ref[...] = _silu(proj(3)).astype(BF16)
    u_ref[...] = (proj(4) * jax.nn.sigmoid(proj(5))).astype(BF16)
    gb_ref[...] = _silu(proj(6)).astype(BF16)


def _in_even(x2, g, w):
    t = x2.shape[0]
    n_in = w.shape[1]
    tok = lambda i: (i, 0)
    fixed = lambda i: (0, 0)
    out = jax.ShapeDtypeStruct((t, NA_W), BF16)
    return pl.pallas_call(
        _in_even_kernel,
        grid=(t // TM,),
        in_specs=[pl.BlockSpec((TM, D_MODEL), tok), pl.BlockSpec((1, D_MODEL), fixed),
                  pl.BlockSpec((D_MODEL, n_in), fixed)],
        out_specs=[pl.BlockSpec((TM, NA_W), tok)] * 6,
        out_shape=[out] * 6,
        compiler_params=_cparams(("parallel",)),
        name="in_even",
    )(x2, g, w)


def _na_bias_table(rpb):
    w = np.arange(GRID_W)
    cs = np.clip(w - NA_WIN_COLS // 2, 0, GRID_W - NA_WIN_COLS)
    kc = np.arange(GRID_W)
    inside = (kc[None, :] >= cs[:, None]) & (kc[None, :] < cs[:, None] + NA_WIN_COLS)
    dc = np.clip(kc[None, :] - w[:, None] + NA_WIN_COLS - 1, 0, 2 * NA_WIN_COLS - 2)
    g = rpb[:, :, dc]
    g = jnp.where(jnp.asarray(inside)[None, None], g, NEG)
    pair = jnp.concatenate([g[:, :-1], g[:, 1:]], axis=-1)
    nd = 2 * NA_WIN_ROWS - 2
    stacked = pair.reshape(NA_HEADS // 2, 2, nd, GRID_W, 2 * GRID_W).transpose(0, 2, 1, 3, 4)
    return stacked.reshape(NA_HEADS // 2 * nd, 2 * GRID_W, 2 * GRID_W).astype(F32)


def _na_kernel(q_ref, kp_ref, kc_ref, kn_ref, vp_ref, vc_ref, vn_ref, ga_ref, tb_ref, o_ref,
               ks_ref, vs_ref, s_ref, p_ref, *, rows):
    blk = NA_ROWS_PER_STEP * GRID_W
    i = pl.program_id(1)
    ks_ref[0:blk] = kp_ref[0]
    ks_ref[blk:2 * blk] = kc_ref[0]
    ks_ref[2 * blk:3 * blk] = kn_ref[0]
    vs_ref[0:blk] = vp_ref[0]
    vs_ref[blk:2 * blk] = vc_ref[0]
    vs_ref[2 * blk:3 * blk] = vn_ref[0]
    lo = lax.broadcasted_iota(jnp.int32, (GRID_W, LANES), 1) < NA_HEAD_DIM
    n_keys = NA_WIN_ROWS * GRID_W
    n_pairs = NA_HEADS // 2
    nd = 2 * NA_WIN_ROWS - 2

    def rows_body(it, carry):
        geo = []
        for sub in range(NA_ROWS_PER_ITER):
            jr = it * NA_ROWS_PER_ITER + sub
            r = i * NA_ROWS_PER_STEP + jr
            rs = jnp.clip(r - NA_WIN_ROWS // 2, 0, rows - NA_WIN_ROWS)
            start = pl.multiple_of((rs - (i - 1) * NA_ROWS_PER_STEP) * GRID_W, GRID_W)
            geo.append((r - rs, start, pl.multiple_of(jr * GRID_W, GRID_W)))
        for sub, (cls, start, q0) in enumerate(geo):
            for pr in range(n_pairs):
                cols = slice(pr * LANES, (pr + 1) * LANES)
                qp = q_ref[0, pl.ds(q0, GRID_W), cols]
                zero = jnp.zeros_like(qp)
                qs = jnp.concatenate([jnp.where(lo, qp, zero), jnp.where(lo, zero, qp)], axis=0)
                base = pr * nd + (NA_WIN_ROWS - 1) - cls
                bias = jnp.concatenate([tb_ref[base + 2 * m] for m in range(NA_WIN_ROWS // 2)], axis=1)
                s_ref[sub * n_pairs + pr] = _dot_nt(qs, ks_ref[pl.ds(start, n_keys), cols]) + bias
        for u in range(NA_ROWS_PER_ITER * n_pairs):
            s = s_ref[u]
            p = jnp.exp(s - jnp.max(s, axis=1, keepdims=True))
            p_ref[u] = (p * (1.0 / jnp.sum(p, axis=1, keepdims=True))).astype(BF16)
        for sub, (cls, start, q0) in enumerate(geo):
            outs = []
            for pr in range(n_pairs):
                cols = slice(pr * LANES, (pr + 1) * LANES)
                o2 = _dot(p_ref[sub * n_pairs + pr], vs_ref[pl.ds(start, n_keys), cols])
                outs.append(jnp.where(lo, o2[0:GRID_W], o2[GRID_W:2 * GRID_W]))
            o = jnp.concatenate(outs, axis=1)
            ga = ga_ref[0, pl.ds(q0, GRID_W), :].astype(F32)
            o_ref[0, pl.ds(q0, GRID_W), :] = (o * ga).astype(BF16)
        return carry

    lax.fori_loop(0, NA_ROWS_PER_STEP // NA_ROWS_PER_ITER, rows_body, 0)


def _natten(q, k, v, ga, table):
    b, s, _ = q.shape
    rows = s // GRID_W
    assert rows % NA_ROWS_PER_STEP == 0 and rows >= NA_WIN_ROWS
    nblk = rows // NA_ROWS_PER_STEP
    blk = NA_ROWS_PER_STEP * GRID_W
    cur = lambda bi, i: (bi, i, 0)
    prev = lambda bi, i: (bi, jnp.maximum(i - 1, 0), 0)
    nxt = lambda bi, i: (bi, jnp.minimum(i + 1, nblk - 1), 0)
    spec = lambda im: pl.BlockSpec((1, blk, NA_W), im)
    return pl.pallas_call(
        functools.partial(_na_kernel, rows=rows),
        grid=(b, nblk),
        in_specs=[spec(cur), spec(prev), spec(cur), spec(nxt), spec(prev), spec(cur), spec(nxt), spec(cur),
                  pl.BlockSpec(table.shape, lambda bi, i: (0, 0, 0))],
        out_specs=spec(cur),
        out_shape=jax.ShapeDtypeStruct((b, s, NA_W), BF16),
        scratch_shapes=[pltpu.VMEM((3 * blk, NA_W), BF16), pltpu.VMEM((3 * blk, NA_W), BF16),
                        pltpu.VMEM((NA_ROWS_PER_ITER * NA_HEADS // 2, 2 * GRID_W, NA_WIN_ROWS * GRID_W), F32),
                        pltpu.VMEM((NA_ROWS_PER_ITER * NA_HEADS // 2, 2 * GRID_W, NA_WIN_ROWS * GRID_W), BF16)],
        compiler_params=_cparams(("parallel", "parallel")),
        name="natten",
    )(q, k, k, k, v, v, v, ga, table)


def _conv_kernel(up_ref, uc_ref, un_ref, gb_ref, w_ref, b_ref, lg_ref, lb_ref, o_ref, slab_ref, sh_ref, *, nblk):
    i = pl.program_id(1)
    ts = CONV_TS
    prev = up_ref[0].astype(F32)
    nxt = un_ref[0].astype(F32)
    slab_ref[0:HALO] = jnp.where(i > 0, prev, jnp.zeros_like(prev))
    slab_ref[HALO:HALO + ts] = uc_ref[0].astype(F32)
    slab_ref[HALO + ts:2 * HALO + ts] = jnp.where(i < nblk - 1, nxt, jnp.zeros_like(nxt))
    span = sh_ref.shape[1]
    for sft in range(1, SUB):
        sh_ref[sft - 1] = slab_ref[sft:sft + span, :]
    first = HALO - CONV_WIDTH // 2
    for rc in range(ts // CONV_RC):
        r0 = rc * CONV_RC
        acc = jnp.broadcast_to(b_ref[...], (CONV_RC, CONV_CH))
        for tap in range(CONV_WIDTH):
            sft = (first + tap) % SUB
            lo = r0 + first + tap - sft
            src = slab_ref[lo:lo + CONV_RC, :] if sft == 0 else sh_ref[sft - 1, lo:lo + CONV_RC, :]
            acc = acc + w_ref[tap:tap + 1, :] * src
        mu = jnp.mean(acc, axis=-1, keepdims=True)
        cen = acc - mu
        var = jnp.mean(cen * cen, axis=-1, keepdims=True)
        y = cen * lax.rsqrt(var + EPS) * lg_ref[...] + lb_ref[...]
        gb = gb_ref[0, r0:r0 + CONV_RC, :].astype(F32)
        o_ref[0, r0:r0 + CONV_RC, :] = (_silu(y) * gb).astype(BF16)


def _conv_branch(u, gb, dw_w, dw_b, cln_g, cln_b):
    b, s, _ = u.shape
    ts = CONV_TS
    nblk = s // ts
    per = ts // HALO
    cur = lambda bi, i: (bi, i, 0)
    fixed = lambda bi, i: (0, 0)
    halo_prev = lambda bi, i: (bi, jnp.maximum(i * per - 1, 0), 0)
    halo_next = lambda bi, i: (bi, jnp.minimum((i + 1) * per, s // HALO - 1), 0)
    vec = pl.BlockSpec((1, CONV_CH), fixed)
    return pl.pallas_call(
        functools.partial(_conv_kernel, nblk=nblk),
        grid=(b, nblk),
        in_specs=[pl.BlockSpec((1, HALO, CONV_CH), halo_prev), pl.BlockSpec((1, ts, CONV_CH), cur),
                  pl.BlockSpec((1, HALO, CONV_CH), halo_next), pl.BlockSpec((1, ts, CONV_CH), cur),
                  pl.BlockSpec((CONV_WIDTH, CONV_CH), fixed), vec, vec, vec],
        out_specs=pl.BlockSpec((1, ts, CONV_CH), cur),
        out_shape=jax.ShapeDtypeStruct((b, s, CONV_CH), BF16),
        scratch_shapes=[pltpu.VMEM((ts + 2 * HALO, CONV_CH), F32),
                        pltpu.VMEM((SUB - 1, ts + 2 * HALO - SUB, CONV_CH), F32)],
        compiler_params=_cparams(("parallel", "parallel")),
        name="conv_branch",
    )(u, u, u, gb, dw_w, dw_b, cln_g, cln_b)


def _tail_kernel(a_ref, c_ref, x_ref, p_ref, wo_ref, g_ref, wg_ref, wp_ref, o_ref):
    half = a_ref.shape[1]
    o = _dot(a_ref[...], wo_ref[0:half, :]) + _dot(c_ref[...], wo_ref[half:2 * half, :])
    x1 = x_ref[...] + _rms(o, g_ref[...])
    gate = jax.nn.sigmoid(_dot(x1.astype(BF16), wg_ref[...]))
    pe = _dot(p_ref[...].astype(BF16), wp_ref[...])
    o_ref[...] = x1 + gate * pe


def _tail(a, c, x2, p, layer, w_out, g_post, w_gate, w_ple):
    t = x2.shape[0]
    half = a.shape[1]
    tok = lambda i: (i, 0)
    fixed = lambda i: (0, 0)
    return pl.pallas_call(
        _tail_kernel,
        grid=(t // TM,),
        in_specs=[pl.BlockSpec((TM, half), tok), pl.BlockSpec((TM, half), tok), pl.BlockSpec((TM, D_MODEL), tok),
                  pl.BlockSpec((None, TM, PLE_DIM), lambda i: (layer, i, 0)),
                  pl.BlockSpec((2 * half, D_MODEL), fixed), pl.BlockSpec((1, D_MODEL), fixed),
                  pl.BlockSpec((D_MODEL, D_MODEL), fixed), pl.BlockSpec((PLE_DIM, D_MODEL), fixed)],
        out_specs=pl.BlockSpec((TM, D_MODEL), tok),
        out_shape=jax.ShapeDtypeStruct((t, D_MODEL), F32),
        compiler_params=_cparams(("parallel",)),
        name="layer_tail",
    )(a, c, x2, p, w_out, g_post, w_gate, w_ple)


_O_CQ = 0
_O_CKV = _O_CQ + Q_LORA
_O_KR = _O_CKV + KV_LORA
_O_GC = _O_KR + LANES
_O_F = _O_GC + MLA_HEADS * MLA_V
_O_GD = _O_F + FNET_GROUPS * FNET_GROUP_DIM
_O_END = _O_GD + FNET_GROUPS * FNET_GROUP_DIM
HEAD_PAD = LANES
MLA_VROWS = MLA_V + 16


def _in_odd_kernel(x_ref, g_ref, w_ref, qg_ref, kvg_ref, wuq_ref, wuqs_ref, wuk_ref, wuvt_ref, rope_ref, dft_ref,
                   q_ref, k_ref, vt_ref, gc_ref, zr_ref, zi_ref, gd_ref):
    h = _rms(x_ref[...], g_ref[...]).astype(BF16)

    def proj(lo, hi):
        return _dot(h, w_ref[:, lo:hi])

    cos_q = rope_ref[:, 0:LANES]
    sin_q = rope_ref[:, LANES:2 * LANES]
    cos_k = rope_ref[:, 2 * LANES:3 * LANES]
    sin_k = rope_ref[:, 3 * LANES:4 * LANES]

    cq = _rms(proj(_O_CQ, _O_CKV), qg_ref[...]).astype(BF16)
    qm = _dot(cq, wuq_ref[...])
    qs = _dot(cq, wuqs_ref[...])
    for hd in range(MLA_HEADS):
        cols = slice(hd * HEAD_PAD, (hd + 1) * HEAD_PAD)
        q_ref[:, cols] = (qm[:, cols] * cos_q + qs[:, cols] * sin_q).astype(BF16)

    ckv_kr = proj(_O_CKV, _O_GC)
    ckv = _rms(ckv_kr[:, 0:KV_LORA], kvg_ref[...]).astype(BF16)
    kr = ckv_kr[:, KV_LORA:KV_LORA + LANES]
    half = MLA_ROPE // 2
    lane = lax.broadcasted_iota(jnp.int32, kr.shape, 1)
    kr_swapped = jnp.where(lane < MLA_NOPE + half, -pltpu.roll(kr, LANES - half, 1), pltpu.roll(kr, half, 1))
    krot = kr * cos_k + kr_swapped * sin_k
    kn = _dot(ckv, wuk_ref[...])
    for hd in range(MLA_HEADS):
        cols = slice(hd * HEAD_PAD, (hd + 1) * HEAD_PAD)
        k_ref[:, cols] = (kn[:, cols] + krot).astype(BF16)
    vt = _dot_nt(wuvt_ref[...], ckv)
    is_one = lax.broadcasted_iota(jnp.int32, vt.shape, 0) % MLA_VROWS >= MLA_V
    vt_ref[0] = jnp.where(is_one, 1.0, vt).astype(BF16)

    gc_ref[...] = _silu(proj(_O_GC, _O_F)).astype(BF16)
    gd_ref[...] = _silu(proj(_O_GD, _O_END)).astype(BF16)
    f = proj(_O_F, _O_GD).astype(BF16)
    for grp in range(FNET_GROUPS):
        cols = slice(grp * FNET_GROUP_DIM, (grp + 1) * FNET_GROUP_DIM)
        z = _dot(f[:, cols], dft_ref[...])
        zr_ref[:, cols] = z[:, 0:FNET_GROUP_DIM].astype(BF16)
        zi_ref[:, cols] = z[:, FNET_GROUP_DIM:2 * FNET_GROUP_DIM].astype(BF16)


def _in_odd(x2, seq, g, w, qg, kvg, wuq, wuqs, wuk, wuvt, rope_tab, dft_c):
    t = x2.shape[0]
    per_seq = seq // TM
    tok = lambda i: (i, 0)
    fixed = lambda i: (0, 0)
    full = lambda a: pl.BlockSpec(a.shape, fixed)
    o1024 = jax.ShapeDtypeStruct((t, MLA_HEADS * HEAD_PAD), BF16)
    o512 = jax.ShapeDtypeStruct((t, 512), BF16)
    ovt = jax.ShapeDtypeStruct((t // seq, MLA_HEADS * MLA_VROWS, seq), BF16)
    s1024 = pl.BlockSpec((TM, MLA_HEADS * HEAD_PAD), tok)
    s512 = pl.BlockSpec((TM, 512), tok)
    svt = pl.BlockSpec((1, MLA_HEADS * MLA_VROWS, TM), lambda i: (i // per_seq, 0, i % per_seq))
    return pl.pallas_call(
        _in_odd_kernel,
        grid=(t // TM,),
        in_specs=[pl.BlockSpec((TM, D_MODEL), tok), full(g), full(w), full(qg), full(kvg), full(wuq), full(wuqs),
                  full(wuk), full(wuvt), pl.BlockSpec((TM, 4 * LANES), lambda i: (i % per_seq, 0)), full(dft_c)],
        out_specs=[s1024, s1024, svt, s512, s512, s512, s512],
        out_shape=[o1024, o1024, ovt, o512, o512, o512, o512],
        compiler_params=_cparams(("parallel",)),
        name="in_odd",
    )(x2, g, w, qg, kvg, wuq, wuqs, wuk, wuvt, rope_tab, dft_c)


def _mla_kernel(q_ref, k_ref, vt_ref, gc_ref, o_ref, m_ref, acc_ref, st_ref, p_ref, tmp_ref, cm_ref, flag_ref):
    kt = pl.program_id(2)

    def scores(hd):
        cols = slice(hd * HEAD_PAD, (hd + 1) * HEAD_PAD)
        return _dot_nt(k_ref[0, :, cols], q_ref[0, :, cols])

    def renewing_step():
        nslot = st_ref.shape[0]
        for hd in range(nslot - 1):
            st_ref[hd] = scores(hd)
        for hd in range(MLA_HEADS):
            slot = hd % nslot
            ahead = hd + nslot - 1
            if ahead < MLA_HEADS:
                st_ref[ahead % nslot] = scores(ahead)
            rows = slice(hd * MLA_VROWS, (hd + 1) * MLA_VROWS)
            tk = st_ref.shape[1]
            parts = [st_ref[slot, r * SUB:(r + 1) * SUB, :] for r in range(MLA_MAX_CHAINS)]
            for r in range(MLA_MAX_CHAINS, tk // SUB):
                parts[r % MLA_MAX_CHAINS] = jnp.maximum(parts[r % MLA_MAX_CHAINS],
                                                        st_ref[slot, r * SUB:(r + 1) * SUB, :])
            while len(parts) > 1:
                parts = [jnp.maximum(parts[2 * j], parts[2 * j + 1]) for j in range(len(parts) // 2)]
            m_prev = m_ref[hd:hd + 1, :]
            m_new = jnp.maximum(m_prev, jnp.max(parts[0], axis=0, keepdims=True))
            alpha = jnp.exp2(m_prev - m_new)
            for c in range(tk // MLA_CHUNK):
                blk = slice(c * MLA_CHUNK, (c + 1) * MLA_CHUNK)
                p_ref[slot, blk, :] = jnp.exp2(st_ref[slot, blk, :] - m_new).astype(BF16)
            acc_ref[rows, :] = alpha * acc_ref[rows, :] + _dot(vt_ref[0, rows, :], p_ref[slot])
            m_ref[hd:hd + 1, :] = m_new

    def streaming_step():
        for hd in range(MLA_HEADS):
            rows = slice(hd * MLA_VROWS, (hd + 1) * MLA_VROWS)
            st = scores(hd)
            cm_ref[hd:hd + 1, :] = jnp.max(st, axis=0, keepdims=True)
            p = jnp.exp2(st - m_ref[hd:hd + 1, :]).astype(BF16)
            tmp_ref[rows, :] = _dot(vt_ref[0, rows, :], p)
        excess = jnp.max(cm_ref[...] - m_ref[...])
        flag_ref[0] = (excess > MLA_FAST_MARGIN).astype(jnp.int32)

    @pl.when(kt == 0)
    def _():
        probes = [_dot_nt(k_ref[0, 0:MLA_PROBE_KEYS, hd * HEAD_PAD:(hd + 1) * HEAD_PAD],
                          q_ref[0, :, hd * HEAD_PAD:(hd + 1) * HEAD_PAD]) for hd in range(MLA_HEADS)]
        m_ref[...] = jnp.concatenate([jnp.max(pr, axis=0, keepdims=True) for pr in probes], axis=0)
        acc_ref[...] = jnp.zeros(acc_ref.shape, F32)

    streaming_step()

    @pl.when(flag_ref[0] == 0)
    def _():
        acc_ref[...] = acc_ref[...] + tmp_ref[...]

    @pl.when(flag_ref[0] != 0)
    def _():
        renewing_step()

    @pl.when(kt == pl.num_programs(2) - 1)
    def _():
        parts = []
        for hd in range(MLA_HEADS):
            num = acc_ref[hd * MLA_VROWS:hd * MLA_VROWS + MLA_V, :]
            den = acc_ref[hd * MLA_VROWS + MLA_V:hd * MLA_VROWS + MLA_V + 1, :]
            parts.append(num / den)
        out = jnp.concatenate(parts, axis=0).T
        o_ref[0] = (out * gc_ref[0].astype(F32)).astype(BF16)


def _mla(q, k, vt, gc):
    b, s, _ = q.shape
    tq, tk = min(MLA_TQ, s), min(MLA_TK, s)
    qmap = lambda bi, qi, ki: (bi, qi, 0)
    kmap = lambda bi, qi, ki: (bi, ki, 0)
    return pl.pallas_call(
        _mla_kernel,
        grid=(b, s // tq, s // tk),
        in_specs=[pl.BlockSpec((1, tq, MLA_HEADS * HEAD_PAD), qmap), pl.BlockSpec((1, tk, MLA_HEADS * HEAD_PAD), kmap),
                  pl.BlockSpec((1, MLA_HEADS * MLA_VROWS, tk), lambda bi, qi, ki: (bi, 0, ki)),
                  pl.BlockSpec((1, tq, MLA_HEADS * MLA_V), qmap)],
        out_specs=pl.BlockSpec((1, tq, MLA_HEADS * MLA_V), qmap),
        out_shape=jax.ShapeDtypeStruct((b, s, MLA_HEADS * MLA_V), BF16),
        scratch_shapes=[pltpu.VMEM((MLA_HEADS, tq), F32), pltpu.VMEM((MLA_HEADS * MLA_VROWS, tq), F32),
                        pltpu.VMEM((MLA_SLOTS, tk, tq), F32), pltpu.VMEM((MLA_SLOTS, tk, tq), BF16),
                        pltpu.VMEM((MLA_HEADS * MLA_VROWS, tq), F32), pltpu.VMEM((MLA_HEADS, tq), F32),
                        pltpu.SMEM((1,), jnp.int32)],
        compiler_params=_cparams(("parallel", "parallel", "arbitrary")),
        name="mla_attention",
    )(q, k, vt, gc)


FNET_COPY_ROWS = 512
FNET_UNROLL = 16


def _fnet_kernel(zr_ref, zi_ref, gd_ref, m1_ref, m2_ref, o_ref, re_ref, im_ref, y_ref):
    seq = re_ref.shape[0]
    n2 = FNET_N2
    n1 = seq // n2

    def copy_in(c, carry):
        rows = pl.ds(pl.multiple_of(c * FNET_COPY_ROWS, FNET_COPY_ROWS), FNET_COPY_ROWS)
        re_ref[rows, :] = zr_ref[0, rows, :].astype(F32)
        im_ref[rows, :] = zi_ref[0, rows, :].astype(F32)
        return carry

    lax.fori_loop(0, seq // FNET_COPY_ROWS, copy_in, 0)

    def stage1(it, carry):
        for u in range(FNET_UNROLL):
            col = it * FNET_UNROLL + u
            rows = pl.ds(col, n1, stride=n2)
            z = jnp.concatenate([re_ref[rows, :], im_ref[rows, :]], axis=0).astype(BF16)
            b = _dot(m1_ref[col], z)
            re_ref[rows, :] = b[0:n1]
            im_ref[rows, :] = b[n1:2 * n1]
        return carry

    lax.fori_loop(0, n2 // FNET_UNROLL, stage1, 0)

    unroll2 = min(FNET_UNROLL, n1)

    def stage2(it, carry):
        for u in range(unroll2):
            k1 = it * unroll2 + u
            rows = pl.ds(pl.multiple_of(k1 * n2, n2), n2)
            z = jnp.concatenate([re_ref[rows, :], im_ref[rows, :]], axis=0).astype(BF16)
            y_ref[pl.ds(k1, n2, stride=n1), :] = _dot(m2_ref[...], z)
        return carry

    lax.fori_loop(0, n1 // unroll2, stage2, 0)

    def gate_out(c, carry):
        rows = pl.ds(pl.multiple_of(c * FNET_COPY_ROWS, FNET_COPY_ROWS), FNET_COPY_ROWS)
        o_ref[0, rows, :] = (y_ref[rows, :] * gd_ref[0, rows, :].astype(F32)).astype(BF16)
        return carry

    lax.fori_loop(0, seq // FNET_COPY_ROWS, gate_out, 0)


def _fnet_tables(seq):
    n2 = FNET_N2
    n1 = seq // n2
    k1 = np.arange(n1)[None, :, None]
    tok = np.arange(n1)[None, None, :] * n2 + np.arange(n2)[:, None, None]
    t = 2.0 * np.pi * ((k1 * tok) % seq) / seq
    c, s = np.cos(t) / np.sqrt(n1), np.sin(t) / np.sqrt(n1)
    m1 = np.concatenate([np.concatenate([c, s], axis=2), np.concatenate([-s, c], axis=2)], axis=1)
    k2 = np.arange(n2)
    a2 = 2.0 * np.pi * np.outer(k2, k2) / n2
    m2 = np.concatenate([np.cos(a2), np.sin(a2)], axis=1) / np.sqrt(n2)
    return jnp.asarray(m1, BF16), jnp.asarray(m2, BF16)


def _fnet(zr, zi, gd):
    b, s, width = zr.shape
    n1 = s // FNET_N2
    assert s % FNET_COPY_ROWS == 0 and n1 % min(FNET_UNROLL, n1) == 0 and FNET_N2 % FNET_UNROLL == 0 and n1 % SUB == 0
    m1, m2 = _fnet_tables(s)
    dat = pl.BlockSpec((1, s, FNET_GROUP_DIM), lambda bi, g: (bi, 0, g))
    return pl.pallas_call(
        _fnet_kernel,
        grid=(b, width // FNET_GROUP_DIM),
        in_specs=[dat, dat, dat, pl.BlockSpec(m1.shape, lambda bi, g: (0, 0, 0)),
                  pl.BlockSpec(m2.shape, lambda bi, g: (0, 0))],
        out_specs=dat,
        out_shape=jax.ShapeDtypeStruct((b, s, width), BF16),
        scratch_shapes=[pltpu.VMEM((s, FNET_GROUP_DIM), F32)] * 3,
        compiler_params=_cparams(("parallel", "parallel")),
        name="fnet",
    )(zr, zi, gd, m1, m2)


def _odd_weights(w_in, w_uq, w_ukv):
    half = MLA_ROPE // 2
    o_kr = Q_LORA + KV_LORA
    o_gc = o_kr + MLA_ROPE
    kr = w_in[:, o_kr:o_gc]
    zeros = lambda n: jnp.zeros((D_MODEL, n), w_in.dtype)
    pad_kr = lambda a: jnp.concatenate([zeros(MLA_NOPE), a, zeros(HEAD_PAD - MLA_NOPE - MLA_ROPE)], axis=1)
    w = jnp.concatenate([w_in[:, :o_kr], pad_kr(kr), w_in[:, o_gc:]], axis=1)
    assert w.shape[1] == _O_END

    uq = w_uq.reshape(Q_LORA, MLA_HEADS, MLA_NOPE + MLA_ROPE)
    rope = uq[:, :, MLA_NOPE:]
    rope_swapped = jnp.concatenate([-rope[:, :, half:], rope[:, :, :half]], axis=2)
    zpad = jnp.zeros((Q_LORA, MLA_HEADS, HEAD_PAD - MLA_NOPE - MLA_ROPE), w_uq.dtype)
    znope = jnp.zeros((Q_LORA, MLA_HEADS, MLA_NOPE), w_uq.dtype)
    wuq = jnp.concatenate([uq, zpad], axis=2).reshape(Q_LORA, MLA_HEADS * HEAD_PAD)
    wuqs = jnp.concatenate([znope, rope_swapped, zpad], axis=2).reshape(Q_LORA, MLA_HEADS * HEAD_PAD)

    ukv = w_ukv.reshape(KV_LORA, MLA_HEADS, MLA_NOPE + MLA_V)
    zk = jnp.zeros((KV_LORA, MLA_HEADS, HEAD_PAD - MLA_NOPE), w_ukv.dtype)
    wuk = jnp.concatenate([ukv[:, :, :MLA_NOPE], zk], axis=2).reshape(KV_LORA, MLA_HEADS * HEAD_PAD)
    zv = jnp.zeros((KV_LORA, MLA_HEADS, MLA_VROWS - MLA_V), w_ukv.dtype)
    wuvt = jnp.concatenate([ukv[:, :, MLA_NOPE:], zv], axis=2).reshape(KV_LORA, MLA_HEADS * MLA_VROWS).T
    return tuple(a.astype(BF16) for a in (w, wuq, wuqs, wuk, wuvt))


def _rope_table(seq):
    pos = jnp.arange(seq, dtype=F32)
    inv_freq = ROPE_THETA ** (-jnp.arange(0, MLA_ROPE, 2, dtype=F32) / MLA_ROPE)
    ang = pos[:, None] * inv_freq[None, :]
    cos2 = jnp.concatenate([jnp.cos(ang)] * 2, axis=1)
    sin2 = jnp.concatenate([jnp.sin(ang)] * 2, axis=1)
    tail = jnp.zeros((seq, HEAD_PAD - MLA_NOPE - MLA_ROPE), F32)
    head0 = jnp.zeros((seq, MLA_NOPE), F32)
    qscale = (MLA_NOPE + MLA_ROPE) ** -0.5 * LOG2E
    cos_q = jnp.concatenate([jnp.ones((seq, MLA_NOPE), F32), cos2, tail], axis=1) * qscale
    sin_q = jnp.concatenate([head0, sin2, tail], axis=1) * qscale
    cos_k = jnp.concatenate([head0, cos2, tail], axis=1)
    sin_k = jnp.concatenate([head0, sin2, tail], axis=1)
    return jnp.concatenate([cos_q, sin_q, cos_k, sin_k], axis=1)


def _channel_dft():
    k = np.arange(FNET_GROUP_DIM)
    a = 2.0 * np.pi * np.outer(k, k) / FNET_GROUP_DIM
    m = np.concatenate([np.cos(a), -np.sin(a)], axis=1) / np.sqrt(FNET_GROUP_DIM)
    return jnp.asarray(m, BF16)


def _trunk(x, p, prm):
    b, s, _ = x.shape
    t = b * s
    x2 = x.reshape(t, D_MODEL)
    p2 = p.reshape(p.shape[0], t, PLE_DIM)
    row = lambda a: a.reshape(1, -1)

    q, k, v, ga, u, gb = _in_even(x2, row(prm["g_pre"][0]), prm["w_in_e"])
    sh = lambda a: a.reshape(b, s, a.shape[-1])
    a_out = _natten(sh(q), sh(k), sh(v), sh(ga), prm["na_table"])
    c_out = _conv_branch(sh(u), sh(gb), prm["dw_w"], row(prm["dw_b"]), row(prm["cln_g"]), row(prm["cln_b"]))
    x2 = _tail(a_out.reshape(t, NA_W), c_out.reshape(t, CONV_CH), x2, p2, 0, prm["w_out_e"],
               row(prm["g_post"][0]), prm["w_gate"][0], prm["w_ple"][0])

    wi, wuq, wuqs, wuk, wuvt = prm["odd"]
    qh, kh, vt, gc, zr, zi, gd = _in_odd(x2, s, row(prm["g_pre"][1]), wi, row(prm["q_norm_g"]), row(prm["kv_norm_g"]),
                                         wuq, wuqs, wuk, wuvt, _rope_table(s), prm["dft_c"])
    m_out = _mla(sh(qh), sh(kh), vt, sh(gc))
    f_out = _fnet(sh(zr), sh(zi), sh(gd))
    x2 = _tail(m_out.reshape(t, MLA_HEADS * MLA_V), f_out.reshape(t, FNET_GROUPS * FNET_GROUP_DIM), x2, p2, 1,
               prm["w_out_o"], row(prm["g_post"][1]), prm["w_gate"][1], prm["w_ple"][1])
    return x2.reshape(b, s, D_MODEL)


def kernel(x_prompt, x_sample, p_prompt, p_sample, g_pre, g_post, w_ple, w_ple_gate, w_in_e, rpb, dw_w, dw_b, cln_g, cln_b, w_out_e, w_in_o, q_norm_g, kv_norm_g, w_uq, w_ukv, w_out_o):
    assert g_pre.shape[0] == 2, "one neighbourhood/conv layer followed by one latent-attention/Fourier layer"
    prm = dict(
        g_pre=g_pre, g_post=g_post, w_ple=w_ple.astype(BF16), w_gate=w_ple_gate.astype(BF16),
        w_in_e=w_in_e[0].astype(BF16), na_table=_na_bias_table(rpb[0]), dw_w=dw_w[0], dw_b=dw_b[0],
        cln_g=cln_g[0], cln_b=cln_b[0], w_out_e=w_out_e[0].astype(BF16),
        odd=_odd_weights(w_in_o[0], w_uq[0], w_ukv[0]), q_norm_g=q_norm_g[0], kv_norm_g=kv_norm_g[0],
        w_out_o=w_out_o[0].astype(BF16), dft_c=_channel_dft(),
    )
    return (_trunk(x_prompt, p_prompt, prm), _trunk(x_sample, p_sample, prm))
```

```python
import functools

import numpy as np
import jax
import jax.numpy as jnp
from jax import lax
from jax.experimental import pallas as pl
from jax.experimental.pallas import tpu as pltpu

F32 = jnp.float32
BF16 = jnp.bfloat16

D_MODEL = 1024
GRID_W = 64
NA_HEADS = 8
NA_HEAD_DIM = 64
NA_WIN_ROWS = 8
NA_WIN_COLS = 16
NA_W = NA_HEADS * NA_HEAD_DIM
CONV_CH = 512
CONV_WIDTH = 31
MLA_HEADS = 8
MLA_NOPE = 64
MLA_ROPE = 32
MLA_V = 64
Q_LORA = 256
KV_LORA = 128
ROPE_THETA = 10000.0
FNET_GROUPS = 4
FNET_GROUP_DIM = 128
PLE_DIM = 256
EPS = 1e-6

LANES = 128
HALO = 16
NEG = -1e30
LOG2E = 1.4426950408889634
VMEM_LIMIT = 56 * 1024 * 1024

TM = 1024
NA_ROWS_PER_STEP = 8
NA_ROWS_PER_ITER = 8
CONV_TS = 512
CONV_RC = 64
MLA_TQ = 512
MLA_TK = 2048
MLA_SLOTS = 3
MLA_FAST_LIMIT = 2.0 ** 80
MLA_PROBE_KEYS = 16
MLA_CHUNK = 64
MLA_MAX_CHAINS = 4
SUB = 8
FNET_N2 = 128


def _cparams(sem):
    return pltpu.CompilerParams(dimension_semantics=sem, vmem_limit_bytes=VMEM_LIMIT)


def _dot(a, b):
    return jnp.dot(a, b, preferred_element_type=F32)


def _dot_nt(a, b):
    return lax.dot_general(a, b, (((1,), (1,)), ((), ())), preferred_element_type=F32)


def _silu(x):
    return x * jax.nn.sigmoid(x)


def _rms(x, g):
    return x * lax.rsqrt(jnp.mean(x * x, axis=-1, keepdims=True) + EPS) * g


def _in_even_kernel(x_ref, g_ref, w_ref, q_ref, k_ref, v_ref, ga_ref, u_ref, gb_ref):
    h = _rms(x_ref[...], g_ref[...]).astype(BF16)

    def proj(c):
        return _dot(h, w_ref[:, c * NA_W:(c + 1) * NA_W])

    q_ref[...] = (proj(0) * (NA_HEAD_DIM ** -0.5)).astype(BF16)
    k_ref[...] = proj(1).astype(BF16)
    v_ref[...] = proj(2).astype(BF16)
    ga_ref[...] = _silu(proj(3)).astype(BF16)
    u_ref[...] = (proj(4) * jax.nn.sigmoid(proj(5))).astype(BF16)
    gb_ref[...] = _silu(proj(6)).astype(BF16)


def _in_even(x2, g, w):
    t = x2.shape[0]
    n_in = w.shape[1]
    tok = lambda i: (i, 0)
    fixed = lambda i: (0, 0)
    out = jax.ShapeDtypeStruct((t, NA_W), BF16)
    return pl.pallas_call(
        _in_even_kernel,
        grid=(t // TM,),
        in_specs=[pl.BlockSpec((TM, D_MODEL), tok), pl.BlockSpec((1, D_MODEL), fixed),
                  pl.BlockSpec((D_MODEL, n_in), fixed)],
        out_specs=[pl.BlockSpec((TM, NA_W), tok)] * 6,
        out_shape=[out] * 6,
        compiler_params=_cparams(("parallel",)),
        name="in_even",
    )(x2, g, w)


def _na_bias_table(rpb):
    w = np.arange(GRID_W)
    cs = np.clip(w - NA_WIN_COLS // 2, 0, GRID_W - NA_WIN_COLS)
    kc = np.arange(GRID_W)
    inside = (kc[None, :] >= cs[:, None]) & (kc[None, :] < cs[:, None] + NA_WIN_COLS)
    dc = np.clip(kc[None, :] - w[:, None] + NA_WIN_COLS - 1, 0, 2 * NA_WIN_COLS - 2)
    g = rpb[:, :, dc]
    g = jnp.where(jnp.asarray(inside)[None, None], g, NEG)
    pair = jnp.concatenate([g[:, :-1], g[:, 1:]], axis=-1)
    nd = 2 * NA_WIN_ROWS - 2
    stacked = pair.reshape(NA_HEADS // 2, 2, nd, GRID_W, 2 * GRID_W).transpose(0, 2, 1, 3, 4)
    return stacked.reshape(NA_HEADS // 2 * nd, 2 * GRID_W, 2 * GRID_W).astype(F32)


def _na_kernel(q_ref, kp_ref, kc_ref, kn_ref, vp_ref, vc_ref, vn_ref, ga_ref, tb_ref, o_ref,
               ks_ref, vs_ref, s_ref, p_ref, *, rows):
    blk = NA_ROWS_PER_STEP * GRID_W
    i = pl.program_id(1)
    ks_ref[0:blk] = kp_ref[0]
    ks_ref[blk:2 * blk] = kc_ref[0]
    ks_ref[2 * blk:3 * blk] = kn_ref[0]
    vs_ref[0:blk] = vp_ref[0]
    vs_ref[blk:2 * blk] = vc_ref[0]
    vs_ref[2 * blk:3 * blk] = vn_ref[0]
    lo = lax.broadcasted_iota(jnp.int32, (GRID_W, LANES), 1) < NA_HEAD_DIM
    n_keys = NA_WIN_ROWS * GRID_W
    n_pairs = NA_HEADS // 2
    nd = 2 * NA_WIN_ROWS - 2

    def rows_body(it, carry):
        geo = []
        for sub in range(NA_ROWS_PER_ITER):
            jr = it * NA_ROWS_PER_ITER + sub
            r = i * NA_ROWS_PER_STEP + jr
            rs = jnp.clip(r - NA_WIN_ROWS // 2, 0, rows - NA_WIN_ROWS)
            start = pl.multiple_of((rs - (i - 1) * NA_ROWS_PER_STEP) * GRID_W, GRID_W)
            geo.append((r - rs, start, pl.multiple_of(jr * GRID_W, GRID_W)))
        for sub, (cls, start, q0) in enumerate(geo):
            for pr in range(n_pairs):
                cols = slice(pr * LANES, (pr + 1) * LANES)
                qp = q_ref[0, pl.ds(q0, GRID_W), cols]
                zero = jnp.zeros_like(qp)
                qs = jnp.concatenate([jnp.where(lo, qp, zero), jnp.where(lo, zero, qp)], axis=0)
                base = pr * nd + (NA_WIN_ROWS - 1) - cls
                bias = jnp.concatenate([tb_ref[base + 2 * m] for m in range(NA_WIN_ROWS // 2)], axis=1)
                s_ref[sub * n_pairs + pr] = _dot_nt(qs, ks_ref[pl.ds(start, n_keys), cols]) + bias
        for u in range(NA_ROWS_PER_ITER * n_pairs):
            s = s_ref[u]
            p = jnp.exp(s - jnp.max(s, axis=1, keepdims=True))
            p_ref[u] = (p * (1.0 / jnp.sum(p, axis=1, keepdims=True))).astype(BF16)
        for sub, (cls, start, q0) in enumerate(geo):
            outs = []
            for pr in range(n_pairs):
                cols = slice(pr * LANES, (pr + 1) * LANES)
                o2 = _dot(p_ref[sub * n_pairs + pr], vs_ref[pl.ds(start, n_keys), cols])
                outs.append(jnp.where(lo, o2[0:GRID_W], o2[GRID_W:2 * GRID_W]))
            o = jnp.concatenate(outs, axis=1)
            ga = ga_ref[0, pl.ds(q0, GRID_W), :].astype(F32)
            o_ref[0, pl.ds(q0, GRID_W), :] = (o * ga).astype(BF16)
        return carry

    lax.fori_loop(0, NA_ROWS_PER_STEP // NA_ROWS_PER_ITER, rows_body, 0)


def _natten(q, k, v, ga, table):
    b, s, _ = q.shape
    rows = s // GRID_W
    assert rows % NA_ROWS_PER_STEP == 0 and rows >= NA_WIN_ROWS
    nblk = rows // NA_ROWS_PER_STEP
    blk = NA_ROWS_PER_STEP * GRID_W
    cur = lambda bi, i: (bi, i, 0)
    prev = lambda bi, i: (bi, jnp.maximum(i - 1, 0), 0)
    nxt = lambda bi, i: (bi, jnp.minimum(i + 1, nblk - 1), 0)
    spec = lambda im: pl.BlockSpec((1, blk, NA_W), im)
    return pl.pallas_call(
        functools.partial(_na_kernel, rows=rows),
        grid=(b, nblk),
        in_specs=[spec(cur), spec(prev), spec(cur), spec(nxt), spec(prev), spec(cur), spec(nxt), spec(cur),
                  pl.BlockSpec(table.shape, lambda bi, i: (0, 0, 0))],
        out_specs=spec(cur),
        out_shape=jax.ShapeDtypeStruct((b, s, NA_W), BF16),
        scratch_shapes=[pltpu.VMEM((3 * blk, NA_W), BF16), pltpu.VMEM((3 * blk, NA_W), BF16),
                        pltpu.VMEM((NA_ROWS_PER_ITER * NA_HEADS // 2, 2 * GRID_W, NA_WIN_ROWS * GRID_W), F32),
                        pltpu.VMEM((NA_ROWS_PER_ITER * NA_HEADS // 2, 2 * GRID_W, NA_WIN_ROWS * GRID_W), BF16)],
        compiler_params=_cparams(("parallel", "parallel")),
        name="natten",
    )(q, k, k, k, v, v, v, ga, table)


def _conv_kernel(up_ref, uc_ref, un_ref, gb_ref, w_ref, b_ref, lg_ref, lb_ref, o_ref, slab_ref, sh_ref, *, nblk):
    i = pl.program_id(1)
    ts = CONV_TS
    prev = up_ref[0].astype(F32)
    nxt = un_ref[0].astype(F32)
    slab_ref[0:HALO] = jnp.where(i > 0, prev, jnp.zeros_like(prev))
    slab_ref[HALO:HALO + ts] = uc_ref[0].astype(F32)
    slab_ref[HALO + ts:2 * HALO + ts] = jnp.where(i < nblk - 1, nxt, jnp.zeros_like(nxt))
    span = sh_ref.shape[1]
    for sft in range(1, SUB):
        sh_ref[sft - 1] = slab_ref[sft:sft + span, :]
    first = HALO - CONV_WIDTH // 2
    for rc in range(ts // CONV_RC):
        r0 = rc * CONV_RC
        acc = jnp.broadcast_to(b_ref[...], (CONV_RC, CONV_CH))
        for tap in range(CONV_WIDTH):
            sft = (first + tap) % SUB
            lo = r0 + first + tap - sft
            src = slab_ref[lo:lo + CONV_RC, :] if sft == 0 else sh_ref[sft - 1, lo:lo + CONV_RC, :]
            acc = acc + w_ref[tap:tap + 1, :] * src
        mu = jnp.mean(acc, axis=-1, keepdims=True)
        cen = acc - mu
        var = jnp.mean(cen * cen, axis=-1, keepdims=True)
        y = cen * lax.rsqrt(var + EPS) * lg_ref[...] + lb_ref[...]
        gb = gb_ref[0, r0:r0 + CONV_RC, :].astype(F32)
        o_ref[0, r0:r0 + CONV_RC, :] = (_silu(y) * gb).astype(BF16)


def _conv_branch(u, gb, dw_w, dw_b, cln_g, cln_b):
    b, s, _ = u.shape
    ts = CONV_TS
    nblk = s // ts
    per = ts // HALO
    cur = lambda bi, i: (bi, i, 0)
    fixed = lambda bi, i: (0, 0)
    halo_prev = lambda bi, i: (bi, jnp.maximum(i * per - 1, 0), 0)
    halo_next = lambda bi, i: (bi, jnp.minimum((i + 1) * per, s // HALO - 1), 0)
    vec = pl.BlockSpec((1, CONV_CH), fixed)
    return pl.pallas_call(
        functools.partial(_conv_kernel, nblk=nblk),
        grid=(b, nblk),
        in_specs=[pl.BlockSpec((1, HALO, CONV_CH), halo_prev), pl.BlockSpec((1, ts, CONV_CH), cur),
                  pl.BlockSpec((1, HALO, CONV_CH), halo_next), pl.BlockSpec((1, ts, CONV_CH), cur),
                  pl.BlockSpec((CONV_WIDTH, CONV_CH), fixed), vec, vec, vec],
        out_specs=pl.BlockSpec((1, ts, CONV_CH), cur),
        out_shape=jax.ShapeDtypeStruct((b, s, CONV_CH), BF16),
        scratch_shapes=[pltpu.VMEM((ts + 2 * HALO, CONV_CH), F32),
                        pltpu.VMEM((SUB - 1, ts + 2 * HALO - SUB, CONV_CH), F32)],
        compiler_params=_cparams(("parallel", "parallel")),
        name="conv_branch",
    )(u, u, u, gb, dw_w, dw_b, cln_g, cln_b)


def _tail_kernel(a_ref, c_ref, x_ref, p_ref, wo_ref, g_ref, wg_ref, wp_ref, o_ref):
    half = a_ref.shape[1]
    o = _dot(a_ref[...], wo_ref[0:half, :]) + _dot(c_ref[...], wo_ref[half:2 * half, :])
    x1 = x_ref[...] + _rms(o, g_ref[...])
    gate = jax.nn.sigmoid(_dot(x1.astype(BF16), wg_ref[...]))
    pe = _dot(p_ref[...].astype(BF16), wp_ref[...])
    o_ref[...] = x1 + gate * pe


def _tail(a, c, x2, p, layer, w_out, g_post, w_gate, w_ple):
    t = x2.shape[0]
    half = a.shape[1]
    tok = lambda i: (i, 0)
    fixed = lambda i: (0, 0)
    return pl.pallas_call(
        _tail_kernel,
        grid=(t // TM,),
        in_specs=[pl.BlockSpec((TM, half), tok), pl.BlockSpec((TM, half), tok), pl.BlockSpec((TM, D_MODEL), tok),
                  pl.BlockSpec((None, TM, PLE_DIM), lambda i: (layer, i, 0)),
                  pl.BlockSpec((2 * half, D_MODEL), fixed), pl.BlockSpec((1, D_MODEL), fixed),
                  pl.BlockSpec((D_MODEL, D_MODEL), fixed), pl.BlockSpec((PLE_DIM, D_MODEL), fixed)],
        out_specs=pl.BlockSpec((TM, D_MODEL), tok),
        out_shape=jax.ShapeDtypeStruct((t, D_MODEL), F32),
        compiler_params=_cparams(("parallel",)),
        name="layer_tail",
    )(a, c, x2, p, w_out, g_post, w_gate, w_ple)


_O_CQ = 0
_O_CKV = _O_CQ + Q_LORA
_O_KR = _O_CKV + KV_LORA
_O_GC = _O_KR + LANES
_O_F = _O_GC + MLA_HEADS * MLA_V
_O_GD = _O_F + FNET_GROUPS * FNET_GROUP_DIM
_O_END = _O_GD + FNET_GROUPS * FNET_GROUP_DIM
HEAD_PAD = LANES
MLA_VROWS = MLA_V + 16


def _in_odd_kernel(x_ref, g_ref, w_ref, qg_ref, kvg_ref, wuq_ref, wuqs_ref, wuk_ref, wuvt_ref, rope_ref, dft_ref,
                   q_ref, k_ref, vt_ref, gc_ref, zr_ref, zi_ref, gd_ref):
    h = _rms(x_ref[...], g_ref[...]).astype(BF16)

    def proj(lo, hi):
        return _dot(h, w_ref[:, lo:hi])

    cos_q = rope_ref[:, 0:LANES]
    sin_q = rope_ref[:, LANES:2 * LANES]
    cos_k = rope_ref[:, 2 * LANES:3 * LANES]
    sin_k = rope_ref[:, 3 * LANES:4 * LANES]

    cq = _rms(proj(_O_CQ, _O_CKV), qg_ref[...]).astype(BF16)
    qm = _dot(cq, wuq_ref[...])
    qs = _dot(cq, wuqs_ref[...])
    for hd in range(MLA_HEADS):
        cols = slice(hd * HEAD_PAD, (hd + 1) * HEAD_PAD)
        q_ref[:, cols] = (qm[:, cols] * cos_q + qs[:, cols] * sin_q).astype(BF16)

    ckv_kr = proj(_O_CKV, _O_GC)
    ckv = _rms(ckv_kr[:, 0:KV_LORA], kvg_ref[...]).astype(BF16)
    kr = ckv_kr[:, KV_LORA:KV_LORA + LANES]
    half = MLA_ROPE // 2
    lane = lax.broadcasted_iota(jnp.int32, kr.shape, 1)
    kr_swapped = jnp.where(lane < MLA_NOPE + half, -pltpu.roll(kr, LANES - half, 1), pltpu.roll(kr, half, 1))
    krot = kr * cos_k + kr_swapped * sin_k
    kn = _dot(ckv, wuk_ref[...])
    for hd in range(MLA_HEADS):
        cols = slice(hd * HEAD_PAD, (hd + 1) * HEAD_PAD)
        k_ref[:, cols] = (kn[:, cols] + krot).astype(BF16)
    vt = _dot_nt(wuvt_ref[...], ckv)
    is_one = lax.broadcasted_iota(jnp.int32, vt.shape, 0) % MLA_VROWS >= MLA_V
    vt_ref[0] = jnp.where(is_one, 1.0, vt).astype(BF16)

    gc_ref[...] = _silu(proj(_O_GC, _O_F)).astype(BF16)
    gd_ref[...] = _silu(proj(_O_GD, _O_END)).astype(BF16)
    f = proj(_O_F, _O_GD).astype(BF16)
    for grp in range(FNET_GROUPS):
        cols = slice(grp * FNET_GROUP_DIM, (grp + 1) * FNET_GROUP_DIM)
        z = _dot(f[:, cols], dft_ref[...])
        zr_ref[:, cols] = z[:, 0:FNET_GROUP_DIM].astype(BF16)
        zi_ref[:, cols] = z[:, FNET_GROUP_DIM:2 * FNET_GROUP_DIM].astype(BF16)


def _in_odd(x2, seq, g, w, qg, kvg, wuq, wuqs, wuk, wuvt, rope_tab, dft_c):
    t = x2.shape[0]
    per_seq = seq // TM
    tok = lambda i: (i, 0)
    fixed = lambda i: (0, 0)
    full = lambda a: pl.BlockSpec(a.shape, fixed)
    o1024 = jax.ShapeDtypeStruct((t, MLA_HEADS * HEAD_PAD), BF16)
    o512 = jax.ShapeDtypeStruct((t, 512), BF16)
    ovt = jax.ShapeDtypeStruct((t // seq, MLA_HEADS * MLA_VROWS, seq), BF16)
    s1024 = pl.BlockSpec((TM, MLA_HEADS * HEAD_PAD), tok)
    s512 = pl.BlockSpec((TM, 512), tok)
    svt = pl.BlockSpec((1, MLA_HEADS * MLA_VROWS, TM), lambda i: (i // per_seq, 0, i % per_seq))
    return pl.pallas_call(
        _in_odd_kernel,
        grid=(t // TM,),
        in_specs=[pl.BlockSpec((TM, D_MODEL), tok), full(g), full(w), full(qg), full(kvg), full(wuq), full(wuqs),
                  full(wuk), full(wuvt), pl.BlockSpec((TM, 4 * LANES), lambda i: (i % per_seq, 0)), full(dft_c)],
        out_specs=[s1024, s1024, svt, s512, s512, s512, s512],
        out_shape=[o1024, o1024, ovt, o512, o512, o512, o512],
        compiler_params=_cparams(("parallel",)),
        name="in_odd",
    )(x2, g, w, qg, kvg, wuq, wuqs, wuk, wuvt, rope_tab, dft_c)


def _mla_kernel(q_ref, k_ref, vt_ref, gc_ref, o_ref, m_ref, acc_ref, st_ref, p_ref, tmp_ref, flag_ref):
    kt = pl.program_id(2)

    def scores(hd):
        cols = slice(hd * HEAD_PAD, (hd + 1) * HEAD_PAD)
        return _dot_nt(k_ref[0, :, cols], q_ref[0, :, cols])

    def renewing_step():
        nslot = st_ref.shape[0]
        for hd in range(nslot - 1):
            st_ref[hd] = scores(hd)
        for hd in range(MLA_HEADS):
            slot = hd % nslot
            ahead = hd + nslot - 1
            if ahead < MLA_HEADS:
                st_ref[ahead % nslot] = scores(ahead)
            rows = slice(hd * MLA_VROWS, (hd + 1) * MLA_VROWS)
            tk = st_ref.shape[1]
            parts = [st_ref[slot, r * SUB:(r + 1) * SUB, :] for r in range(MLA_MAX_CHAINS)]
            for r in range(MLA_MAX_CHAINS, tk // SUB):
                parts[r % MLA_MAX_CHAINS] = jnp.maximum(parts[r % MLA_MAX_CHAINS],
                                                        st_ref[slot, r * SUB:(r + 1) * SUB, :])
            while len(parts) > 1:
                parts = [jnp.maximum(parts[2 * j], parts[2 * j + 1]) for j in range(len(parts) // 2)]
            m_prev = m_ref[hd:hd + 1, :]
            m_new = jnp.maximum(m_prev, jnp.max(parts[0], axis=0, keepdims=True))
            alpha = jnp.exp2(m_prev - m_new)
            for c in range(tk // MLA_CHUNK):
                blk = slice(c * MLA_CHUNK, (c + 1) * MLA_CHUNK)
                p_ref[slot, blk, :] = jnp.exp2(st_ref[slot, blk, :] - m_new).astype(BF16)
            acc_ref[rows, :] = alpha * acc_ref[rows, :] + _dot(vt_ref[0, rows, :], p_ref[slot])
            m_ref[hd:hd + 1, :] = m_new

    def streaming_step():
        for hd in range(MLA_HEADS):
            rows = slice(hd * MLA_VROWS, (hd + 1) * MLA_VROWS)
            p = jnp.exp2(scores(hd) - m_ref[hd:hd + 1, :]).astype(BF16)
            tmp_ref[rows, :] = _dot(vt_ref[0, rows, :], p)
        dens = jnp.concatenate([tmp_ref[hd * MLA_VROWS + MLA_V:hd * MLA_VROWS + MLA_V + 1, :]
                                for hd in range(MLA_HEADS)], axis=0)
        rejected = jnp.max(jnp.where(dens <= MLA_FAST_LIMIT, 0.0, 1.0))
        flag_ref[0] = (rejected > 0.0).astype(jnp.int32)

    @pl.when(kt == 0)
    def _():
        probes = [_dot_nt(k_ref[0, 0:MLA_PROBE_KEYS, hd * HEAD_PAD:(hd + 1) * HEAD_PAD],
                          q_ref[0, :, hd * HEAD_PAD:(hd + 1) * HEAD_PAD]) for hd in range(MLA_HEADS)]
        m_ref[...] = jnp.concatenate([jnp.max(pr, axis=0, keepdims=True) for pr in probes], axis=0)
        acc_ref[...] = jnp.zeros(acc_ref.shape, F32)

    streaming_step()

    @pl.when(flag_ref[0] == 0)
    def _():
        acc_ref[...] = acc_ref[...] + tmp_ref[...]

    @pl.when(flag_ref[0] != 0)
    def _():
        renewing_step()

    @pl.when(kt == pl.num_programs(2) - 1)
    def _():
        parts = []
        for hd in range(MLA_HEADS):
            num = acc_ref[hd * MLA_VROWS:hd * MLA_VROWS + MLA_V, :]
            den = acc_ref[hd * MLA_VROWS + MLA_V:hd * MLA_VROWS + MLA_V + 1, :]
            parts.append(num / den)
        out = jnp.concatenate(parts, axis=0).T
        o_ref[0] = (out * gc_ref[0].astype(F32)).astype(BF16)


def _mla(q, k, vt, gc):
    b, s, _ = q.shape
    tq, tk = min(MLA_TQ, s), min(MLA_TK, s)
    qmap = lambda bi, qi, ki: (bi, qi, 0)
    kmap = lambda bi, qi, ki: (bi, ki, 0)
    return pl.pallas_call(
        _mla_kernel,
        grid=(b, s // tq, s // tk),
        in_specs=[pl.BlockSpec((1, tq, MLA_HEADS * HEAD_PAD), qmap), pl.BlockSpec((1, tk, MLA_HEADS * HEAD_PAD), kmap),
                  pl.BlockSpec((1, MLA_HEADS * MLA_VROWS, tk), lambda bi, qi, ki: (bi, 0, ki)),
                  pl.BlockSpec((1, tq, MLA_HEADS * MLA_V), qmap)],
        out_specs=pl.BlockSpec((1, tq, MLA_HEADS * MLA_V), qmap),
        out_shape=jax.ShapeDtypeStruct((b, s, MLA_HEADS * MLA_V), BF16),
        scratch_shapes=[pltpu.VMEM((MLA_HEADS, tq), F32), pltpu.VMEM((MLA_HEADS * MLA_VROWS, tq), F32),
                        pltpu.VMEM((MLA_SLOTS, tk, tq), F32), pltpu.VMEM((MLA_SLOTS, tk, tq), BF16),
                        pltpu.VMEM((MLA_HEADS * MLA_VROWS, tq), F32), pltpu.SMEM((1,), jnp.int32)],
        compiler_params=_cparams(("parallel", "parallel", "arbitrary")),
        name="mla_attention",
    )(q, k, vt, gc)


FNET_COPY_ROWS = 512
FNET_UNROLL = 16


def _fnet_kernel(zr_ref, zi_ref, gd_ref, m1_ref, m2_ref, o_ref, re_ref, im_ref, y_ref):
    seq = re_ref.shape[0]
    n2 = FNET_N2
    n1 = seq // n2

    def copy_in(c, carry):
        rows = pl.ds(pl.multiple_of(c * FNET_COPY_ROWS, FNET_COPY_ROWS), FNET_COPY_ROWS)
        re_ref[rows, :] = zr_ref[0, rows, :].astype(F32)
        im_ref[rows, :] = zi_ref[0, rows, :].astype(F32)
        return carry

    lax.fori_loop(0, seq // FNET_COPY_ROWS, copy_in, 0)

    def stage1(it, carry):
        for u in range(FNET_UNROLL):
            col = it * FNET_UNROLL + u
            rows = pl.ds(col, n1, stride=n2)
            z = jnp.concatenate([re_ref[rows, :], im_ref[rows, :]], axis=0).astype(BF16)
            b = _dot(m1_ref[col], z)
            re_ref[rows, :] = b[0:n1]
            im_ref[rows, :] = b[n1:2 * n1]
        return carry

    lax.fori_loop(0, n2 // FNET_UNROLL, stage1, 0)

    unroll2 = min(FNET_UNROLL, n1)

    def stage2(it, carry):
        for u in range(unroll2):
            k1 = it * unroll2 + u
            rows = pl.ds(pl.multiple_of(k1 * n2, n2), n2)
            z = jnp.concatenate([re_ref[rows, :], im_ref[rows, :]], axis=0).astype(BF16)
            y_ref[pl.ds(k1, n2, stride=n1), :] = _dot(m2_ref[...], z)
        return carry

    lax.fori_loop(0, n1 // unroll2, stage2, 0)

    def gate_out(c, carry):
        rows = pl.ds(pl.multiple_of(c * FNET_COPY_ROWS, FNET_COPY_ROWS), FNET_COPY_ROWS)
        o_ref[0, rows, :] = (y_ref[rows, :] * gd_ref[0, rows, :].astype(F32)).astype(BF16)
        return carry

    lax.fori_loop(0, seq // FNET_COPY_ROWS, gate_out, 0)


def _fnet_tables(seq):
    n2 = FNET_N2
    n1 = seq // n2
    k1 = np.arange(n1)[None, :, None]
    tok = np.arange(n1)[None, None, :] * n2 + np.arange(n2)[:, None, None]
    t = 2.0 * np.pi * ((k1 * tok) % seq) / seq
    c, s = np.cos(t) / np.sqrt(n1), np.sin(t) / np.sqrt(n1)
    m1 = np.concatenate([np.concatenate([c, s], axis=2), np.concatenate([-s, c], axis=2)], axis=1)
    k2 = np.arange(n2)
    a2 = 2.0 * np.pi * np.outer(k2, k2) / n2
    m2 = np.concatenate([np.cos(a2), np.sin(a2)], axis=1) / np.sqrt(n2)
    return jnp.asarray(m1, BF16), jnp.asarray(m2, BF16)


def _fnet(zr, zi, gd):
    b, s, width = zr.shape
    n1 = s // FNET_N2
    assert s % FNET_COPY_ROWS == 0 and n1 % min(FNET_UNROLL, n1) == 0 and FNET_N2 % FNET_UNROLL == 0 and n1 % SUB == 0
    m1, m2 = _fnet_tables(s)
    dat = pl.BlockSpec((1, s, FNET_GROUP_DIM), lambda bi, g: (bi, 0, g))
    return pl.pallas_call(
        _fnet_kernel,
        grid=(b, width // FNET_GROUP_DIM),
        in_specs=[dat, dat, dat, pl.BlockSpec(m1.shape, lambda bi, g: (0, 0, 0)),
                  pl.BlockSpec(m2.shape, lambda bi, g: (0, 0))],
        out_specs=dat,
        out_shape=jax.ShapeDtypeStruct((b, s, width), BF16),
        scratch_shapes=[pltpu.VMEM((s, FNET_GROUP_DIM), F32)] * 3,
        compiler_params=_cparams(("parallel", "parallel")),
        name="fnet",
    )(zr, zi, gd, m1, m2)


def _odd_weights(w_in, w_uq, w_ukv):
    half = MLA_ROPE // 2
    o_kr = Q_LORA + KV_LORA
    o_gc = o_kr + MLA_ROPE
    kr = w_in[:, o_kr:o_gc]
    zeros = lambda n: jnp.zeros((D_MODEL, n), w_in.dtype)
    pad_kr = lambda a: jnp.concatenate([zeros(MLA_NOPE), a, zeros(HEAD_PAD - MLA_NOPE - MLA_ROPE)], axis=1)
    w = jnp.concatenate([w_in[:, :o_kr], pad_kr(kr), w_in[:, o_gc:]], axis=1)
    assert w.shape[1] == _O_END

    uq = w_uq.reshape(Q_LORA, MLA_HEADS, MLA_NOPE + MLA_ROPE)
    rope = uq[:, :, MLA_NOPE:]
    rope_swapped = jnp.concatenate([-rope[:, :, half:], rope[:, :, :half]], axis=2)
    zpad = jnp.zeros((Q_LORA, MLA_HEADS, HEAD_PAD - MLA_NOPE - MLA_ROPE), w_uq.dtype)
    znope = jnp.zeros((Q_LORA, MLA_HEADS, MLA_NOPE), w_uq.dtype)
    wuq = jnp.concatenate([uq, zpad], axis=2).reshape(Q_LORA, MLA_HEADS * HEAD_PAD)
    wuqs = jnp.concatenate([znope, rope_swapped, zpad], axis=2).reshape(Q_LORA, MLA_HEADS * HEAD_PAD)

    ukv = w_ukv.reshape(KV_LORA, MLA_HEADS, MLA_NOPE + MLA_V)
    zk = jnp.zeros((KV_LORA, MLA_HEADS, HEAD_PAD - MLA_NOPE), w_ukv.dtype)
    wuk = jnp.concatenate([ukv[:, :, :MLA_NOPE], zk], axis=2).reshape(KV_LORA, MLA_HEADS * HEAD_PAD)
    zv = jnp.zeros((KV_LORA, MLA_HEADS, MLA_VROWS - MLA_V), w_ukv.dtype)
    wuvt = jnp.concatenate([ukv[:, :, MLA_NOPE:], zv], axis=2).reshape(KV_LORA, MLA_HEADS * MLA_VROWS).T
    return tuple(a.astype(BF16) for a in (w, wuq, wuqs, wuk, wuvt))


def _rope_table(seq):
    pos = jnp.arange(seq, dtype=F32)
    inv_freq = ROPE_THETA ** (-jnp.arange(0, MLA_ROPE, 2, dtype=F32) / MLA_ROPE)
    ang = pos[:, None] * inv_freq[None, :]
    cos2 = jnp.concatenate([jnp.cos(ang)] * 2, axis=1)
    sin2 = jnp.concatenate([jnp.sin(ang)] * 2, axis=1)
    tail = jnp.zeros((seq, HEAD_PAD - MLA_NOPE - MLA_ROPE), F32)
    head0 = jnp.zeros((seq, MLA_NOPE), F32)
    qscale = (MLA_NOPE + MLA_ROPE) ** -0.5 * LOG2E
    cos_q = jnp.concatenate([jnp.ones((seq, MLA_NOPE), F32), cos2, tail], axis=1) * qscale
    sin_q = jnp.concatenate([head0, sin2, tail], axis=1) * qscale
    cos_k = jnp.concatenate([head0, cos2, tail], axis=1)
    sin_k = jnp.concatenate([head0, sin2, tail], axis=1)
    return jnp.concatenate([cos_q, sin_q, cos_k, sin_k], axis=1)


def _channel_dft():
    k = np.arange(FNET_GROUP_DIM)
    a = 2.0 * np.pi * np.outer(k, k) / FNET_GROUP_DIM
    m = np.concatenate([np.cos(a), -np.sin(a)], axis=1) / np.sqrt(FNET_GROUP_DIM)
    return jnp.asarray(m, BF16)


def _trunk(x, p, prm):
    b, s, _ = x.shape
    t = b * s
    x2 = x.reshape(t, D_MODEL)
    p2 = p.reshape(p.shape[0], t, PLE_DIM)
    row = lambda a: a.reshape(1, -1)

    q, k, v, ga, u, gb = _in_even(x2, row(prm["g_pre"][0]), prm["w_in_e"])
    sh = lambda a: a.reshape(b, s, a.shape[-1])
    a_out = _natten(sh(q), sh(k), sh(v), sh(ga), prm["na_table"])
    c_out = _conv_branch(sh(u), sh(gb), prm["dw_w"], row(prm["dw_b"]), row(prm["cln_g"]), row(prm["cln_b"]))
    x2 = _tail(a_out.reshape(t, NA_W), c_out.reshape(t, CONV_CH), x2, p2, 0, prm["w_out_e"],
               row(prm["g_post"][0]), prm["w_gate"][0], prm["w_ple"][0])

    wi, wuq, wuqs, wuk, wuvt = prm["odd"]
    qh, kh, vt, gc, zr, zi, gd = _in_odd(x2, s, row(prm["g_pre"][1]), wi, row(prm["q_norm_g"]), row(prm["kv_norm_g"]),
                                         wuq, wuqs, wuk, wuvt, _rope_table(s), prm["dft_c"])
    m_out = _mla(sh(qh), sh(kh), vt, sh(gc))
    f_out = _fnet(sh(zr), sh(zi), sh(gd))
    x2 = _tail(m_out.reshape(t, MLA_HEADS * MLA_V), f_out.reshape(t, FNET_GROUPS * FNET_GROUP_DIM), x2, p2, 1,
               prm["w_out_o"], row(prm["g_post"][1]), prm["w_gate"][1], prm["w_ple"][1])
    return x2.reshape(b, s, D_MODEL)


def kernel(x_prompt, x_sample, p_prompt, p_sample, g_pre, g_post, w_ple, w_ple_gate, w_in_e, rpb, dw_w, dw_b, cln_g, cln_b, w_out_e, w_in_o, q_norm_g, kv_norm_g, w_uq, w_ukv, w_out_o):
    assert g_pre.shape[0] == 2, "one neighbourhood/conv layer followed by one latent-attention/Fourier layer"
    prm = dict(
        g_pre=g_pre, g_post=g_post, w_ple=w_ple.astype(BF16), w_gate=w_ple_gate.astype(BF16),
        w_in_e=w_in_e[0].astype(BF16), na_table=_na_bias_table(rpb[0]), dw_w=dw_w[0], dw_b=dw_b[0],
        cln_g=cln_g[0], cln_b=cln_b[0], w_out_e=w_out_e[0].astype(BF16),
        odd=_odd_weights(w_in_o[0], w_uq[0], w_ukv[0]), q_norm_g=q_norm_g[0], kv_norm_g=kv_norm_g[0],
        w_out_o=w_out_o[0].astype(BF16), dft_c=_channel_dft(),
    )
    return (_trunk(x_prompt, p_prompt, prm), _trunk(x_sample, p_sample, prm))
```
